```python
import jax, jax.numpy as jnp
from jax import lax
import numpy as np

D_MODEL = 2048
BATCH = 8
SEQ = 4096
DEPTH = 4

HG_WIDTH = D_MODEL // 4
MLA_WIDTH = D_MODEL // 2
RW_WIDTH = D_MODEL // 4
MIX_WIDTH = HG_WIDTH + MLA_WIDTH + RW_WIDTH

HG_HEAD_DIM = 128
HG_HEADS = HG_WIDTH // HG_HEAD_DIM
HG_CHUNK = 64

MLA_NOPE = 128
MLA_ROPE = 64
MLA_V = 128
MLA_HEADS = MLA_WIDTH // MLA_V
MLA_Q_RANK = 512
MLA_KV_RANK = 256
MLA_QK_DIM = MLA_NOPE + MLA_ROPE
ROPE_THETA = 10000.0
Q_BLOCK = 128
MASK_VALUE = -1e30

RW_HEAD = 64
RW_HEADS = RW_WIDTH // RW_HEAD
RW_W_RANK = 64
RW_A_RANK = 64
RW_V_RANK = 32
RW_G_RANK = 128
RW_GN_EPS = 64e-5
RW_SIZES = (RW_WIDTH, RW_WIDTH, RW_WIDTH, RW_W_RANK, RW_A_RANK, RW_G_RANK)
RW_IN = sum(RW_SIZES)

IN_SIZES = (HG_WIDTH, HG_WIDTH, HG_WIDTH, HG_WIDTH, MLA_Q_RANK, MLA_KV_RANK, MLA_ROPE, RW_IN)
IN_WIDTH = sum(IN_SIZES)

D_FF = ((8 * D_MODEL + 3 * 256 - 1) // (3 * 256)) * 256
RMS_EPS = 1e-6

kernel_name = "hymba_style_hgrn2_mla_rwkv7_hybrid"


def split_cols(t, sizes):
    out, start = [], 0
    for s in sizes:
        out.append(t[..., start:start + s])
        start += s
    return out


def rms_norm(x, gain):
    xf = x.astype(jnp.float32)
    y = xf * lax.rsqrt(jnp.mean(xf * xf, axis=-1, keepdims=True) + RMS_EPS)
    return (y * gain.astype(jnp.float32)).astype(x.dtype)


def rope_tables(positions):
    half = MLA_ROPE // 2
    inv_freq = ROPE_THETA ** (-jnp.arange(half, dtype=jnp.float32) / half)
    ang = positions.astype(jnp.float32)[..., None] * inv_freq
    return jnp.cos(ang), jnp.sin(ang)


def apply_rope(x, cos, sin):
    x1, x2 = jnp.split(x.astype(jnp.float32), 2, axis=-1)
    return jnp.concatenate([x1 * cos - x2 * sin, x2 * cos + x1 * sin], axis=-1).astype(x.dtype)


def hgrn2_chunkwise(q, log_f, k, v):
    B, S, H, DK = q.shape
    DV = v.shape[-1]
    nc = S // HG_CHUNK
    to_chunks = lambda t: t.reshape(B, nc, HG_CHUNK, H, t.shape[-1]).transpose(1, 0, 3, 2, 4)
    causal = jnp.tril(jnp.ones((HG_CHUNK, HG_CHUNK), dtype=bool))[:, :, None]

    def step(state, inp):
        q_c, lf_c, k_c, v_c = inp
        b = jnp.cumsum(lf_c, axis=2)
        o_inter = jnp.einsum('bhtk,bhkv->bhtv', q_c * jnp.exp(b), state)
        diff = b[:, :, :, None, :] - b[:, :, None, :, :]
        decay = jnp.where(causal, jnp.exp(jnp.where(causal, diff, 0.0)), 0.0)
        scores = jnp.einsum('bhtk,bhsk,bhtsk->bhts', q_c, k_c, decay)
        o = o_inter + jnp.einsum('bhts,bhsv->bhtv', scores, v_c)
        b_last = b[:, :, -1:, :]
        state = state * jnp.exp(b_last[:, :, 0, :, None]) + jnp.einsum(
            'bhsk,bhsv->bhkv', k_c * jnp.exp(b_last - b), v_c)
        return state, o

    s0 = jnp.zeros((B, H, DK, DV), jnp.float32)
    _, o = lax.scan(step, s0, (to_chunks(q), to_chunks(log_f), to_chunks(k), to_chunks(v)))
    return o.transpose(1, 0, 3, 2, 4).reshape(B, S, H, DV)


def hgrn2_mixer(q_raw, f_raw, i_raw, g_raw, lower_bound, out_gain):
    B, S, _ = q_raw.shape
    shp = (B, S, HG_HEADS, HG_HEAD_DIM)
    q = jax.nn.silu(q_raw.astype(jnp.float32)).reshape(shp)
    z = f_raw.astype(jnp.float32).reshape(shp)
    lb = lower_bound.astype(jnp.float32).reshape(HG_HEADS, HG_HEAD_DIM)
    log_f = jnp.log(lb + (1.0 - lb) * jax.nn.sigmoid(z))
    k = (1.0 - lb) * jax.nn.sigmoid(-z)
    v = i_raw.astype(jnp.float32).reshape(shp)
    o = hgrn2_chunkwise(q, log_f, k, v)
    o = rms_norm(o, out_gain) * jax.nn.silu(g_raw.astype(jnp.float32)).reshape(shp)
    return o.reshape(B, S, HG_WIDTH).astype(q_raw.dtype)


def causal_block_attention(q_nope, q_rope, k_nope, k_rope, v):
    B, S, H, _ = q_nope.shape
    nb = S // Q_BLOCK
    blocks = lambda t: jnp.moveaxis(t.reshape(B, nb, Q_BLOCK, *t.shape[2:]), 1, 0)
    k_idx = jnp.arange(S)
    scale = MLA_QK_DIM ** -0.5

    def one_block(args):
        qn, qr, blk = args
        s = (jnp.einsum('bqhd,bkhd->bhqk', qn, k_nope)
             + jnp.einsum('bqhd,bkd->bhqk', qr, k_rope))
        q_idx = blk * Q_BLOCK + jnp.arange(Q_BLOCK)
        s = jnp.where(k_idx[None, :] <= q_idx[:, None], s.astype(jnp.float32) * scale, MASK_VALUE)
        p = jax.nn.softmax(s, axis=-1)
        return jnp.einsum('bhqk,bkhd->bqhd', p.astype(v.dtype), v)

    out = lax.map(one_block, (blocks(q_nope), blocks(q_rope), jnp.arange(nb)))
    return jnp.moveaxis(out, 0, 1).reshape(B, S, H * v.shape[-1])


def mla_mixer(q_a, kv_a, k_rope_raw, cos, sin, q_a_gain, w_qb, kv_a_gain, w_kvb, q_gain, k_gain):
    B, S, _ = q_a.shape
    q = (rms_norm(q_a, q_a_gain) @ w_qb).reshape(B, S, MLA_HEADS, MLA_QK_DIM)
    kv = (rms_norm(kv_a, kv_a_gain) @ w_kvb).reshape(B, S, MLA_HEADS, MLA_NOPE + MLA_V)
    q_nope, q_rope = q[..., :MLA_NOPE], q[..., MLA_NOPE:]
    k_nope, v = kv[..., :MLA_NOPE], kv[..., MLA_NOPE:]
    q_nope = rms_norm(q_nope, q_gain[:MLA_NOPE])
    q_rope = rms_norm(q_rope, q_gain[MLA_NOPE:])
    k_nope = rms_norm(k_nope, k_gain[:MLA_NOPE])
    k_rope = rms_norm(k_rope_raw, k_gain[MLA_NOPE:])
    q_rope = apply_rope(q_rope, cos[:, :, None, :], sin[:, :, None, :])
    k_rope = apply_rope(k_rope, cos, sin)
    o = causal_block_attention(q_nope, q_rope, k_nope, k_rope, v)
    return o.astype(q_a.dtype)


def rwkv7_mixer(proj, mu, w0, w2, a0, a2, g2, k_k, k_a, r_k, ln_w, ln_b, v_first, v_res):
    B, S, _ = proj.shape
    p = proj.astype(jnp.float32)
    prev = jnp.pad(p, ((0, 0), (1, 0), (0, 0)))[:, :S]
    p = p + (prev - p) * mu
    r, k, v, wd, ad, gd = split_cols(p, RW_SIZES)
    w_log = -jax.nn.softplus(-(w0 + jnp.tanh(wd) @ w2)) - 0.5
    a = jax.nn.sigmoid(a0 + ad @ a2)
    g = jax.nn.sigmoid(gd) @ g2
    if v_res is None:
        v_first = v
    else:
        v0, v1, v2 = v_res
        v = v + (v_first - v) * jax.nn.sigmoid(v0 + (v @ v1) @ v2)
    heads = lambda t: t.reshape(B, S, RW_HEADS, RW_HEAD)
    kk = heads(k * k_k)
    kk = kk / jnp.maximum(jnp.sqrt(jnp.sum(kk * kk, axis=-1, keepdims=True)), 1e-12)
    k = heads(k * (1.0 + (a - 1.0) * k_a))
    r, v, a = heads(r), heads(v), heads(a)
    decay = jnp.exp(-jnp.exp(heads(w_log)))
    time_major = lambda t: jnp.moveaxis(t, 1, 0)

    def step(state, inp):
        r_t, w_t, k_t, v_t, kk_t, a_t = inp
        sa = jnp.einsum('bhvk,bhk->bhv', state, -kk_t)
        state = (state * w_t[:, :, None, :]
                 + sa[..., :, None] * (kk_t * a_t)[..., None, :]
                 + v_t[..., :, None] * k_t[..., None, :])
        return state, jnp.einsum('bhvk,bhk->bhv', state, r_t)

    s0 = jnp.zeros((B, RW_HEADS, RW_HEAD, RW_HEAD), jnp.float32)
    _, y = lax.scan(step, s0, (time_major(r), time_major(decay), time_major(k),
                               time_major(v), time_major(kk), time_major(a)))
    y = jnp.moveaxis(y, 0, 1)
    mean = jnp.mean(y, axis=-1, keepdims=True)
    var = jnp.mean(jnp.square(y - mean), axis=-1, keepdims=True)
    y = ((y - mean) * lax.rsqrt(var + RW_GN_EPS)).reshape(B, S, RW_WIDTH) * ln_w + ln_b
    bonus = (jnp.sum(r * k * r_k, axis=-1, keepdims=True) * v).reshape(B, S, RW_WIDTH)
    out = (y + bonus) * g
    return out.astype(proj.dtype), v_first


def setup_inputs(seed: int = 0) -> dict:
    key = jax.random.key(seed)
    ks = iter(jax.random.split(key, 40))
    nrm = lambda shape, scale: jax.random.normal(next(ks), shape, jnp.float32) * scale
    gain = lambda shape: 1.0 + nrm(shape, 0.02)
    L = DEPTH
    return {
        "x": nrm((BATCH, SEQ, D_MODEL), 1.0),
        "positions": jnp.arange(SEQ, dtype=jnp.int32)[None, :]
                     + jax.random.randint(next(ks), (BATCH, 1), 0, 64, dtype=jnp.int32),
        "attn_norm": gain((L, D_MODEL)),
        "w_in": nrm((L, D_MODEL, IN_WIDTH), D_MODEL ** -0.5),
        "hg_lower_bounds": nrm((L, HG_WIDTH), 0.1),
        "hg_out_norm": gain((L, HG_HEAD_DIM)),
        "mla_q_a_norm": gain((L, MLA_Q_RANK)),
        "mla_w_qb": nrm((L, MLA_Q_RANK, MLA_HEADS * MLA_QK_DIM), MLA_Q_RANK ** -0.5),
        "mla_kv_a_norm": gain((L, MLA_KV_RANK)),
        "mla_w_kvb": nrm((L, MLA_KV_RANK, MLA_HEADS * (MLA_NOPE + MLA_V)), MLA_KV_RANK ** -0.5),
        "mla_q_norm": gain((L, MLA_QK_DIM)),
        "mla_k_norm": gain((L, MLA_QK_DIM)),
        "rw_mu": jax.random.uniform(next(ks), (L, RW_IN), jnp.float32),
        "rw_w0": jax.random.uniform(next(ks), (L, RW_WIDTH), jnp.float32, -5.0, 1.0),
        "rw_w2": nrm((L, RW_W_RANK, RW_WIDTH), RW_W_RANK ** -0.5),
        "rw_a0": nrm((L, RW_WIDTH), 0.1),
        "rw_a2": nrm((L, RW_A_RANK, RW_WIDTH), RW_A_RANK ** -0.5),
        "rw_g2": nrm((L, RW_G_RANK, RW_WIDTH), RW_G_RANK ** -0.5),
        "rw_v0": nrm((L - 1, RW_WIDTH), 0.1),
        "rw_v1": nrm((L - 1, RW_WIDTH, RW_V_RANK), RW_WIDTH ** -0.5),
        "rw_v2": nrm((L - 1, RW_V_RANK, RW_WIDTH), RW_V_RANK ** -0.5),
        "rw_k_k": 0.85 + nrm((L, RW_WIDTH), 0.02),
        "rw_k_a": gain((L, RW_WIDTH)),
        "rw_r_k": nrm((L, RW_HEADS, RW_HEAD), 0.1),
        "rw_ln_w": gain((L, RW_WIDTH)),
        "rw_ln_b": nrm((L, RW_WIDTH), 0.02),
        "w_o": nrm((L, MIX_WIDTH, D_MODEL), MIX_WIDTH ** -0.5),
        "ffn_norm": gain((L, D_MODEL)),
        "w_gate_up": nrm((L, D_MODEL, 2 * D_FF), D_MODEL ** -0.5),
        "w_down": nrm((L, D_FF, D_MODEL), D_FF ** -0.5),
    }


def reference(x, positions, attn_norm, w_in, hg_lower_bounds, hg_out_norm,
              mla_q_a_norm, mla_w_qb, mla_kv_a_norm, mla_w_kvb, mla_q_norm, mla_k_norm,
              rw_mu, rw_w0, rw_w2, rw_a0, rw_a2, rw_g2, rw_v0, rw_v1, rw_v2,
              rw_k_k, rw_k_a, rw_r_k, rw_ln_w, rw_ln_b,
              w_o, ffn_norm, w_gate_up, w_down):
    lb_sm = jax.nn.softmax(hg_lower_bounds.astype(jnp.float32), axis=0)
    lower_bounds = jnp.cumsum(lb_sm, axis=0) - lb_sm[0]
    cos, sin = rope_tables(positions)
    v_first = None
    for l in range(DEPTH):
        h = rms_norm(x, attn_norm[l])
        p = h @ w_in[l]
        hq, hf, hi, hg, mq, mkv, mkr, rw = split_cols(p, IN_SIZES)
        o_hg = hgrn2_mixer(hq, hf, hi, hg, lower_bounds[l], hg_out_norm[l])
        o_mla = mla_mixer(mq, mkv, mkr, cos, sin, mla_q_a_norm[l], mla_w_qb[l],
                          mla_kv_a_norm[l], mla_w_kvb[l], mla_q_norm[l], mla_k_norm[l])
        v_res = None if l == 0 else (rw_v0[l - 1], rw_v1[l - 1], rw_v2[l - 1])
        o_rw, v_first = rwkv7_mixer(rw, rw_mu[l], rw_w0[l], rw_w2[l], rw_a0[l], rw_a2[l], rw_g2[l],
                                    rw_k_k[l], rw_k_a[l], rw_r_k[l], rw_ln_w[l], rw_ln_b[l],
                                    v_first, v_res)
        mixed = jnp.concatenate([o_hg, o_mla, o_rw], axis=-1)
        x = x + (mixed @ w_o[l]).astype(x.dtype)
        h = rms_norm(x, ffn_norm[l])
        gate, up = jnp.split(h @ w_gate_up[l], 2, axis=-1)
        x = x + ((jax.nn.silu(gate) * up) @ w_down[l]).astype(x.dtype)
    return x
```

```python
import functools

import jax
import jax.numpy as jnp
from jax import lax
from jax.experimental import pallas as pl
from jax.experimental.pallas import tpu as pltpu

F32 = jnp.float32
BF16 = jnp.bfloat16
HIGHEST = lax.Precision.HIGHEST

D_MODEL = 2048
DEPTH = 4
HG_WIDTH = 512
HG_HEAD_DIM = 128
HG_HEADS = HG_WIDTH // HG_HEAD_DIM
MLA_NOPE = 128
MLA_ROPE = 64
MLA_V = 128
MLA_HEADS = 8
MLA_WIDTH = MLA_HEADS * MLA_V
MLA_Q_RANK = 512
MLA_KV_RANK = 256
MLA_QK_DIM = MLA_NOPE + MLA_ROPE
MLA_QK_PAD = 256
MLA_IN = MLA_Q_RANK + MLA_KV_RANK + MLA_ROPE
MLA_IN_PAD = 896
ROPE_THETA = 10000.0
MASK_VALUE = -1e30
RW_WIDTH = 512
RW_HEAD = 64
RW_HEADS = RW_WIDTH // RW_HEAD
RW_PAIRS = RW_HEADS // 2
RW_W_RANK = 64
RW_A_RANK = 64
RW_V_RANK = 32
RW_G_RANK = 128
RW_IN = 3 * RW_WIDTH + RW_W_RANK + RW_A_RANK + RW_G_RANK
RW_GN_EPS = 64e-5
MIX_WIDTH = HG_WIDTH + MLA_WIDTH + RW_WIDTH
D_FF = 5632
RMS_EPS = 1e-6

HG_CHUNK = 32
HG_EXP_CLAMP = 80.0
RW_CHUNK = 64
LANES = 128

_MIB = 1024 * 1024


def _cparams(semantics, vmem_mib):
    return pltpu.CompilerParams(dimension_semantics=semantics, vmem_limit_bytes=vmem_mib * _MIB)


def _dot(a, b, precision=None):
    return jnp.dot(a, b, preferred_element_type=F32, precision=precision)


def _dot_nt(a, b, precision=None):
    return lax.dot_general(a, b, (((1,), (1,)), ((), ())), preferred_element_type=F32, precision=precision)


def _dot_tn(a, b, precision=None):
    return lax.dot_general(a, b, (((0,), (0,)), ((), ())), preferred_element_type=F32, precision=precision)


def _rms(x, width):
    ms = jnp.sum(x * x, axis=-1, keepdims=True) * (1.0 / width)
    return x * lax.rsqrt(ms + RMS_EPS)


def _norm_matmul_body(x_ref, g_ref, w_ref, o_ref, h_ref):
    @pl.when(pl.program_id(1) == 0)
    def _():
        h_ref[...] = (_rms(x_ref[...], x_ref.shape[-1]) * g_ref[...]).astype(BF16)

    o_ref[...] = _dot(h_ref[...], w_ref[...]).astype(o_ref.dtype)


def norm_matmul(x, gain, w, *, bm, bn, out_dtype=F32):
    t, d = x.shape
    n = w.shape[1]
    return pl.pallas_call(
        _norm_matmul_body,
        grid=(t // bm, n // bn),
        in_specs=[
            pl.BlockSpec((bm, d), lambda i, j: (i, 0)),
            pl.BlockSpec((1, d), lambda i, j: (0, 0)),
            pl.BlockSpec((d, bn), lambda i, j: (0, j)),
        ],
        out_specs=pl.BlockSpec((bm, bn), lambda i, j: (i, j)),
        out_shape=jax.ShapeDtypeStruct((t, n), out_dtype),
        scratch_shapes=[pltpu.VMEM((bm, d), BF16)],
        compiler_params=_cparams(("parallel", "arbitrary"), 48),
        name="norm_matmul",
    )(x, gain.reshape(1, d), w)


def _out_proj_body(x_ref, a_ref, b_ref, c_ref, wa_ref, wb_ref, wc_ref, o_ref):
    acc = _dot(a_ref[...], wa_ref[...]) + _dot(b_ref[...], wb_ref[...]) + _dot(c_ref[...], wc_ref[...])
    o_ref[...] = x_ref[...] + acc


def out_proj(x, o_hg, o_mla, o_rw, w_hg, w_mla, w_rw, *, bm, bn):
    t, d = x.shape
    row = lambda width: pl.BlockSpec((bm, width), lambda i, j: (i, 0))
    col = lambda width: pl.BlockSpec((width, bn), lambda i, j: (0, j))
    return pl.pallas_call(
        _out_proj_body,
        grid=(t // bm, d // bn),
        in_specs=[pl.BlockSpec((bm, bn), lambda i, j: (i, j)),
                  row(HG_WIDTH), row(MLA_WIDTH), row(RW_WIDTH),
                  col(HG_WIDTH), col(MLA_WIDTH), col(RW_WIDTH)],
        out_specs=pl.BlockSpec((bm, bn), lambda i, j: (i, j)),
        out_shape=jax.ShapeDtypeStruct((t, d), F32),
        compiler_params=_cparams(("parallel", "arbitrary"), 48),
        name="out_proj",
    )(x, o_hg, o_mla, o_rw, w_hg, w_mla, w_rw)


def _ffn_body(x_ref, g_ref, wg_ref, wu_ref, wd_ref, o_ref, h_ref):
    @pl.when(pl.program_id(1) == 0)
    def _():
        x = x_ref[...]
        h_ref[...] = (_rms(x, x.shape[-1]) * g_ref[...]).astype(BF16)
        o_ref[...] = x

    h = h_ref[...]
    gate = _dot(h, wg_ref[...])
    up = _dot(h, wu_ref[...])
    act = (gate * jax.nn.sigmoid(gate) * up).astype(BF16)
    o_ref[...] += _dot(act, wd_ref[...])


def ffn(x, gain, w_gate_up, w_down, *, bm, bf):
    t, d = x.shape
    dff = w_down.shape[0]
    nf = dff // bf
    return pl.pallas_call(
        _ffn_body,
        grid=(t // bm, nf),
        in_specs=[
            pl.BlockSpec((bm, d), lambda i, j: (i, 0)),
            pl.BlockSpec((1, d), lambda i, j: (0, 0)),
            pl.BlockSpec((d, bf), lambda i, j: (0, j)),
            pl.BlockSpec((d, bf), lambda i, j: (0, j + nf)),
            pl.BlockSpec((bf, d), lambda i, j: (j, 0)),
        ],
        out_specs=pl.BlockSpec((bm, d), lambda i, j: (i, 0)),
        out_shape=jax.ShapeDtypeStruct((t, d), F32),
        scratch_shapes=[pltpu.VMEM((bm, d), BF16)],
        compiler_params=_cparams(("parallel", "arbitrary"), 56),
        name="ffn",
    )(x, gain.reshape(1, d), w_gate_up, w_gate_up, w_down)


def _hgrn2_body(p_ref, lb_ref, gain_ref, o_ref, st_ref):
    c = HG_CHUNK
    w = HG_WIDTH

    @pl.when(pl.program_id(1) == 0)
    def _():
        st_ref[...] = jnp.zeros_like(st_ref)

    row = lax.broadcasted_iota(jnp.int32, (c, c), 0)
    col = lax.broadcasted_iota(jnp.int32, (c, c), 1)
    tril = col <= row
    cum = tril.astype(F32)
    lb = lb_ref[...]
    gain = gain_ref[...]

    def chunk(ci, carry):
        r0 = pl.multiple_of(ci * c, c)
        rows = pl.ds(r0, c)
        q_raw = p_ref[rows, 0:w]
        z = p_ref[rows, w:2 * w]
        v = p_ref[rows, 2 * w:3 * w]
        g_raw = p_ref[rows, 3 * w:4 * w]
        q = q_raw * jax.nn.sigmoid(q_raw)
        log_f = jnp.log(lb + (1.0 - lb) * jax.nn.sigmoid(z))
        k = (1.0 - lb) * jax.nn.sigmoid(-z)
        b = _dot(cum, log_f, HIGHEST)
        b_mid = b[c // 2 - 1:c // 2, :]
        b_end = b[c - 1:c, :]
        q_in = (q * jnp.exp(jnp.minimum(b - b_mid, HG_EXP_CLAMP))).astype(BF16)
        k_in = (k * jnp.exp(jnp.minimum(b_mid - b, HG_EXP_CLAMP))).astype(BF16)
        q_st = (q * jnp.exp(b)).astype(BF16)
        k_st = (k * jnp.exp(b_end - b)).astype(BF16)
        d_end = jnp.exp(b_end)
        vb = v.astype(BF16)
        gate = g_raw * jax.nn.sigmoid(g_raw)
        for h in range(HG_HEADS):
            sl = slice(h * HG_HEAD_DIM, (h + 1) * HG_HEAD_DIM)
            s = jnp.where(tril, _dot_nt(q_in[:, sl], k_in[:, sl]), 0.0)
            st = st_ref[h]
            o = _dot_nt(q_st[:, sl], st.astype(BF16)) + _dot(s.astype(BF16), vb[:, sl])
            st_ref[h] = st * d_end[:, sl] + _dot_tn(vb[:, sl], k_st[:, sl])
            o = _rms(o, HG_HEAD_DIM) * gain * gate[:, sl]
            o_ref[rows, sl] = o.astype(o_ref.dtype)
        return carry

    lax.fori_loop(0, p_ref.shape[0] // c, chunk, 0)


def hgrn2(p_hg, lower_bound, out_gain, *, batch, bs):
    t = p_hg.shape[0]
    ns = t // batch // bs
    return pl.pallas_call(
        _hgrn2_body,
        grid=(batch, ns),
        in_specs=[
            pl.BlockSpec((bs, 4 * HG_WIDTH), lambda b, s: (b * ns + s, 0)),
            pl.BlockSpec((1, HG_WIDTH), lambda b, s: (0, 0)),
            pl.BlockSpec((1, HG_HEAD_DIM), lambda b, s: (0, 0)),
        ],
        out_specs=pl.BlockSpec((bs, HG_WIDTH), lambda b, s: (b * ns + s, 0)),
        out_shape=jax.ShapeDtypeStruct((t, HG_WIDTH), BF16),
        scratch_shapes=[pltpu.VMEM((HG_HEADS, HG_HEAD_DIM, HG_HEAD_DIM), F32)],
        compiler_params=_cparams(("parallel", "arbitrary"), 32),
        name="hgrn2",
    )(p_hg, lower_bound.reshape(1, HG_WIDTH), out_gain.reshape(1, HG_HEAD_DIM))


def _rope(x, cos, sin):
    half = MLA_ROPE // 2
    lane = lax.broadcasted_iota(jnp.int32, x.shape, 1)
    rot = jnp.where(lane < half, -pltpu.roll(x, LANES - half, 1), pltpu.roll(x, half, 1))
    return x * cos + rot * sin


def _mla_prep_body(p_ref, cos_ref, sin_ref, gqa_ref, wqb_ref, gkva_ref, wkvb_ref, gq_ref, gk_ref,
                   q_ref, k_ref, v_ref):
    p = p_ref[...]
    cos = cos_ref[...]
    sin = sin_ref[...]
    scale = MLA_QK_DIM ** -0.5
    qn = (_rms(p[:, 0:MLA_Q_RANK], MLA_Q_RANK) * gqa_ref[...]).astype(BF16)
    q = _dot(qn, wqb_ref[...])
    kvn = (_rms(p[:, MLA_Q_RANK:MLA_Q_RANK + MLA_KV_RANK], MLA_KV_RANK) * gkva_ref[...]).astype(BF16)
    kv = _dot(kvn, wkvb_ref[...])
    gq_nope, gq_rope = gq_ref[0:1, :], gq_ref[1:2, :]
    gk_nope, gk_rope = gk_ref[0:1, :], gk_ref[1:2, :]
    k_rope = _rope(_rms(p[:, MLA_Q_RANK + MLA_KV_RANK:], MLA_ROPE) * gk_rope, cos, sin).astype(BF16)
    for h in range(MLA_HEADS):
        o = h * MLA_QK_PAD
        q_nope = _rms(q[:, o:o + MLA_NOPE], MLA_NOPE) * (gq_nope * scale)
        q_rope = _rope(_rms(q[:, o + MLA_NOPE:o + MLA_QK_PAD], MLA_ROPE) * (gq_rope * scale), cos, sin)
        q_ref[h, :, 0:MLA_NOPE] = q_nope.astype(BF16)
        q_ref[h, :, MLA_NOPE:MLA_QK_PAD] = q_rope.astype(BF16)
        k_ref[h, :, 0:MLA_NOPE] = (_rms(kv[:, o:o + MLA_NOPE], MLA_NOPE) * gk_nope).astype(BF16)
        k_ref[h, :, MLA_NOPE:MLA_QK_PAD] = k_rope
        v_ref[h] = kv[:, o + MLA_NOPE:o + MLA_QK_PAD].astype(BF16)


def mla_prep(p_mla, cos, sin, g_qa, w_qb, g_kva, w_kvb, g_q, g_k, *, bm):
    t = p_mla.shape[0]
    full = lambda a: pl.BlockSpec(a.shape, lambda i: (0,) * a.ndim)
    rows = lambda width: pl.BlockSpec((bm, width), lambda i: (i, 0))
    heads = lambda width: pl.BlockSpec((MLA_HEADS, bm, width), lambda i: (0, i, 0))
    g_qa = g_qa.reshape(1, -1)
    g_kva = g_kva.reshape(1, -1)
    return pl.pallas_call(
        _mla_prep_body,
        grid=(t // bm,),
        in_specs=[rows(MLA_IN_PAD), rows(LANES), rows(LANES), full(g_qa), full(w_qb), full(g_kva), full(w_kvb),
                  full(g_q), full(g_k)],
        out_specs=[heads(MLA_QK_PAD), heads(MLA_QK_PAD), heads(MLA_V)],
        out_shape=[jax.ShapeDtypeStruct((MLA_HEADS, t, MLA_QK_PAD), BF16),
                   jax.ShapeDtypeStruct((MLA_HEADS, t, MLA_QK_PAD), BF16),
                   jax.ShapeDtypeStruct((MLA_HEADS, t, MLA_V), BF16)],
        compiler_params=_cparams(("parallel",), 48),
        name="mla_prep",
    )(p_mla, cos, sin, g_qa, w_qb, g_kva, w_kvb, g_q, g_k)


def _flash_body(q_ref, k_ref, v_ref, o_ref, m_ref, l_ref, acc_ref):
    i = pl.program_id(2)
    j = pl.program_id(3)

    @pl.when(j == 0)
    def _():
        m_ref[...] = jnp.full_like(m_ref, MASK_VALUE)
        l_ref[...] = jnp.zeros_like(l_ref)
        acc_ref[...] = jnp.zeros_like(acc_ref)

    def step(masked):
        s = _dot_nt(q_ref[0], k_ref[0])
        if masked:
            row = lax.broadcasted_iota(jnp.int32, s.shape, 0)
            col = lax.broadcasted_iota(jnp.int32, s.shape, 1)
            s = jnp.where(col <= row, s, MASK_VALUE)
        m_old = m_ref[...]
        m_new = jnp.maximum(m_old, jnp.max(s, axis=-1, keepdims=True))
        alpha = jnp.exp(m_old - m_new)
        p = jnp.exp(s - m_new)
        l_ref[...] = alpha * l_ref[...] + jnp.sum(p, axis=-1, keepdims=True)
        acc_ref[...] = alpha * acc_ref[...] + _dot(p.astype(BF16), v_ref[0])
        m_ref[...] = m_new

    @pl.when(j < i)
    def _():
        step(False)

    @pl.when(j == i)
    def _():
        step(True)
        o_ref[...] = (acc_ref[...] / l_ref[...]).astype(o_ref.dtype)


def flash_attn(q, k, v, *, batch, blk):
    heads, t, _ = q.shape
    nb = t // batch // blk
    qmap = lambda b, h, i, j: (h, b * nb + i, 0)
    kmap = lambda b, h, i, j: (h, b * nb + jnp.minimum(i, j), 0)
    return pl.pallas_call(
        _flash_body,
        grid=(batch, heads, nb, nb),
        in_specs=[pl.BlockSpec((1, blk, MLA_QK_PAD), qmap),
                  pl.BlockSpec((1, blk, MLA_QK_PAD), kmap),
                  pl.BlockSpec((1, blk, MLA_V), kmap)],
        out_specs=pl.BlockSpec((blk, MLA_V), lambda b, h, i, j: (b * nb + i, h)),
        out_shape=jax.ShapeDtypeStruct((t, heads * MLA_V), BF16),
        scratch_shapes=[pltpu.VMEM((blk, 1), F32), pltpu.VMEM((blk, 1), F32), pltpu.VMEM((blk, MLA_V), F32)],
        compiler_params=_cparams(("parallel", "parallel", "parallel", "arbitrary"), 48),
        name="flash_attn",
    )(q, k, v)


def _unit_lower_inverse(a, prec):
    n = a.shape[0]
    row = lax.broadcasted_iota(jnp.int32, (n, n), 0)
    col = lax.broadcasted_iota(jnp.int32, (n, n), 1)
    t = jnp.where(row == col, 1.0, 0.0) + jnp.where((row // 2) == (col // 2), a, 0.0)
    size = 2
    while size < n:
        off = jnp.where(((row // (2 * size)) == (col // (2 * size))) & ((row // size) != (col // size)), a, 0.0)
        t = t + _dot(_dot(t, off, prec), t, prec)
        size *= 2
    return t


def _rw_chunk_body(has_vres, prec, *refs):
    if has_vres:
        (p_ref, prev_ref, mu_ref, w0_ref, w2_ref, a0_ref, a2_ref, g2_ref, kk_ref, ka_ref, rk_ref,
         v0_ref, v1_ref, v2_ref, vf_ref,
         m_ref, n_ref, qh_ref, oh_ref, gate_ref, bonus_ref,
         r_s, g_s, k_s, v_s, kk_s, b_s) = refs
    else:
        (p_ref, prev_ref, mu_ref, w0_ref, w2_ref, a0_ref, a2_ref, g2_ref, kk_ref, ka_ref, rk_ref,
         m_ref, n_ref, qh_ref, oh_ref, gate_ref, bonus_ref, vout_ref,
         r_s, g_s, k_s, v_s, kk_s, b_s) = refs
    c = RW_CHUNK
    w = RW_WIDTH
    bt = p_ref.shape[0]

    p = p_ref[...]
    first = pl.program_id(1) == 0
    prev_row = jnp.where(first, 0.0, prev_ref[7:8, :])
    rowi = lax.broadcasted_iota(jnp.int32, p.shape, 0)
    prev = jnp.where(rowi == 0, prev_row, pltpu.roll(p, 1, 0))
    p = p + (prev - p) * mu_ref[...]
    r = p[:, 0:w]
    k = p[:, w:2 * w]
    v = p[:, 2 * w:3 * w]
    wd = p[:, 3 * w:3 * w + RW_W_RANK]
    ad = p[:, 3 * w + RW_W_RANK:3 * w + RW_W_RANK + RW_A_RANK]
    gd = p[:, 3 * w + RW_W_RANK + RW_A_RANK:]
    w_log = -jax.nn.softplus(-(w0_ref[...] + _dot(jnp.tanh(wd).astype(BF16), w2_ref[...]))) - 0.5
    lr = jax.nn.sigmoid(a0_ref[...] + _dot(ad.astype(BF16), a2_ref[...]))
    gate = _dot(jax.nn.sigmoid(gd).astype(BF16), g2_ref[...])
    if has_vres:
        mix = jax.nn.sigmoid(v0_ref[...] + _dot(_dot(v.astype(BF16), v1_ref[...]).astype(BF16), v2_ref[...]))
        v = v + (vf_ref[...] - v) * mix
    else:
        vout_ref[...] = v
    li = lax.broadcasted_iota(jnp.int32, (w, w), 0) // RW_HEAD
    lj = lax.broadcasted_iota(jnp.int32, (w, w), 1) // RW_HEAD
    head_sum = (li == lj).astype(F32)
    kk = k * kk_ref[...]
    kk = kk / jnp.maximum(jnp.sqrt(_dot(kk * kk, head_sum, HIGHEST)), 1e-12)
    k = k * (1.0 + (lr - 1.0) * ka_ref[...])
    gate_ref[...] = gate
    bonus_ref[...] = _dot(r * k * rk_ref[...], head_sum, HIGHEST) * v
    r_s[...] = r
    g_s[...] = -jnp.exp(w_log)
    k_s[...] = k
    v_s[...] = v
    kk_s[...] = kk
    b_s[...] = kk * lr

    row = lax.broadcasted_iota(jnp.int32, (c, c), 0)
    col = lax.broadcasted_iota(jnp.int32, (c, c), 1)
    incl = col <= row
    strict = col < row
    cum = incl.astype(F32)
    lane = lax.broadcasted_iota(jnp.int32, (1, LANES), 1)
    eye = (lax.broadcasted_iota(jnp.int32, (LANES, LANES), 0) == lax.broadcasted_iota(jnp.int32, (LANES, LANES), 1))

    def chunk(ci, carry):
        rows = pl.ds(pl.multiple_of(ci * c, c), c)
        for pr in range(RW_PAIRS):
            sl = slice(pr * LANES, (pr + 1) * LANES)
            g = g_s[rows, sl]
            gc = _dot(cum, g, HIGHEST)
            g_end = gc[c - 1:c, :]
            kkc = kk_s[rows, sl]
            bc = b_s[rows, sl]
            kc = k_s[rows, sl]
            vc = v_s[rows, sl]
            a_t = -kkc * jnp.exp(gc - g)
            r_t = r_s[rows, sl] * jnp.exp(gc)
            inv = jnp.exp(-gc)
            b_h = bc * inv
            k_h = kc * inv
            to_end = jnp.exp(g_end - gc)
            b_e = bc * to_end
            k_e = kc * to_end
            m_acc = jnp.where(eye, jnp.exp(g_end), 0.0)
            n_acc = jnp.zeros((LANES, LANES), F32)
            qh_acc = jnp.zeros((c, LANES), F32)
            oh_acc = jnp.zeros((c, LANES), F32)
            for e in range(2):
                msk = (lane >= e * RW_HEAD) & (lane < (e + 1) * RW_HEAD)
                a_m = jnp.where(msk, a_t, 0.0)
                r_m = jnp.where(msk, r_t, 0.0)
                v_m = jnp.where(msk, vc, 0.0)
                lhs = jnp.concatenate([a_m, r_m], axis=0)
                ab = _dot_nt(lhs, b_h, prec)
                ak = _dot_nt(lhs, k_h, prec)
                a_ab = jnp.where(strict, ab[0:c], 0.0)
                a_rb = jnp.where(incl, ab[c:2 * c], 0.0)
                a_ak = jnp.where(strict, ak[0:c], 0.0)
                a_rk = jnp.where(incl, ak[c:2 * c], 0.0)
                t_inv = _unit_lower_inverse(a_ab, prec)
                p1 = _dot(t_inv, a_m, prec)
                p2 = _dot(t_inv, _dot(a_ak, v_m, prec), prec)
                qh_acc = qh_acc + r_m + _dot(a_rb, p1, prec)
                oh_acc = oh_acc + _dot(a_rb, p2, prec) + _dot(a_rk, v_m, prec)
                b_m = jnp.where(msk, b_e, 0.0)
                k_m = jnp.where(msk, k_e, 0.0)
                m_acc = m_acc + _dot_tn(b_m, p1, prec)
                n_acc = n_acc + _dot_tn(b_m, p2, prec) + _dot_tn(k_m, v_m, prec)
            m_ref[ci, pr] = m_acc
            n_ref[ci, pr] = n_acc
            qh_ref[rows, sl] = qh_acc
            oh_ref[rows, sl] = oh_acc
        return carry

    lax.fori_loop(0, bt // c, chunk, 0)


def rw_chunk(p_rw, mu, w0, w2, a0, a2, g2, k_k, k_a, r_k, vres, *, batch, bt, prec):
    t = p_rw.shape[0]
    nb = t // batch // bt
    nc = bt // RW_CHUNK
    has_vres = vres is not None
    row1 = lambda a: a.reshape(1, -1)
    full = lambda a: pl.BlockSpec(a.shape, lambda b, s: (0,) * a.ndim)
    rows = lambda width: pl.BlockSpec((bt, width), lambda b, s: (b * nb + s, 0))
    prev_spec = pl.BlockSpec((8, RW_IN), lambda b, s: (jnp.maximum((b * nb + s) * (bt // 8) - 1, 0), 0))
    mats = pl.BlockSpec((nc, RW_PAIRS, LANES, LANES), lambda b, s: (b * nb + s, 0, 0, 0))
    params = [row1(mu), row1(w0), w2, row1(a0), a2, g2, row1(k_k), row1(k_a), row1(r_k)]
    args = [p_rw, p_rw] + params
    in_specs = [rows(RW_IN), prev_spec] + [full(a) for a in params]
    if has_vres:
        v0, v1, v2, v_first = vres
        extra = [row1(v0), v1, v2]
        args += extra + [v_first]
        in_specs += [full(a) for a in extra] + [rows(RW_WIDTH)]
    mat_shape = jax.ShapeDtypeStruct((t // RW_CHUNK, RW_PAIRS, LANES, LANES), F32)
    tok_shape = jax.ShapeDtypeStruct((t, RW_WIDTH), F32)
    out_specs = [mats, mats, rows(RW_WIDTH), rows(RW_WIDTH), rows(RW_WIDTH), rows(RW_WIDTH)]
    out_shape = [mat_shape, mat_shape, tok_shape, tok_shape, tok_shape, tok_shape]
    if not has_vres:
        out_specs.append(rows(RW_WIDTH))
        out_shape.append(tok_shape)
    return pl.pallas_call(
        functools.partial(_rw_chunk_body, has_vres, prec),
        grid=(batch, nb),
        in_specs=in_specs,
        out_specs=out_specs,
        out_shape=out_shape,
        scratch_shapes=[pltpu.VMEM((bt, RW_WIDTH), F32) for _ in range(6)],
        compiler_params=_cparams(("parallel", "arbitrary"), 48),
        name="rw_chunk",
    )(*args)


def _rw_scan_body(prec, m_ref, n_ref, qh_ref, oh_ref, gate_ref, bonus_ref, lnw_ref, lnb_ref, o_ref, h_ref):
    c = RW_CHUNK
    w = RW_WIDTH

    @pl.when(pl.program_id(1) == 0)
    def _():
        h_ref[...] = jnp.zeros_like(h_ref)

    li = lax.broadcasted_iota(jnp.int32, (w, w), 0) // RW_HEAD
    lj = lax.broadcasted_iota(jnp.int32, (w, w), 1) // RW_HEAD
    head_mean = jnp.where(li == lj, 1.0 / RW_HEAD, 0.0)

    def chunk(ci, carry):
        rows = pl.ds(pl.multiple_of(ci * c, c), c)
        ys = []
        for pr in range(RW_PAIRS):
            sl = slice(pr * LANES, (pr + 1) * LANES)
            h = h_ref[pr]
            ys.append(_dot(qh_ref[rows, sl], h, prec) + oh_ref[rows, sl])
            h_ref[pr] = _dot(m_ref[ci, pr], h, prec) + n_ref[ci, pr]
        y = jnp.concatenate(ys, axis=1)
        mean = _dot(y, head_mean, HIGHEST)
        d = y - mean
        var = _dot(d * d, head_mean, HIGHEST)
        y = d * lax.rsqrt(var + RW_GN_EPS) * lnw_ref[...] + lnb_ref[...]
        o_ref[rows, :] = ((y + bonus_ref[rows, :]) * gate_ref[rows, :]).astype(o_ref.dtype)
        return carry

    lax.fori_loop(0, qh_ref.shape[0] // c, chunk, 0)


def rw_scan(m, n, qh, oh, gate, bonus, ln_w, ln_b, *, batch, bs, prec):
    t = qh.shape[0]
    nb = t // batch // bs
    nc = bs // RW_CHUNK
    rows = pl.BlockSpec((bs, RW_WIDTH), lambda b, s: (b * nb + s, 0))
    mats = pl.BlockSpec((nc, RW_PAIRS, LANES, LANES), lambda b, s: (b * nb + s, 0, 0, 0))
    vec = pl.BlockSpec((1, RW_WIDTH), lambda b, s: (0, 0))
    return pl.pallas_call(
        functools.partial(_rw_scan_body, prec),
        grid=(batch, nb),
        in_specs=[mats, mats, rows, rows, rows, rows, vec, vec],
        out_specs=rows,
        out_shape=jax.ShapeDtypeStruct((t, RW_WIDTH), BF16),
        scratch_shapes=[pltpu.VMEM((RW_PAIRS, LANES, LANES), F32)],
        compiler_params=_cparams(("parallel", "arbitrary"), 48),
        name="rw_scan",
    )(m, n, qh, oh, gate, bonus, ln_w.reshape(1, -1), ln_b.reshape(1, -1))


def _rope_tables(positions):
    half = MLA_ROPE // 2
    inv_freq = ROPE_THETA ** (-jnp.arange(half, dtype=F32) / half)
    ang = positions.astype(F32).reshape(-1, 1) * inv_freq
    zeros = jnp.zeros((ang.shape[0], LANES - MLA_ROPE), F32)
    cos = jnp.concatenate([jnp.cos(ang), jnp.cos(ang), zeros], axis=1)
    sin = jnp.concatenate([jnp.sin(ang), jnp.sin(ang), zeros], axis=1)
    return cos, sin


def _pad_cols(a, width):
    return jnp.pad(a, [(0, 0)] * (a.ndim - 1) + [(0, width - a.shape[-1])])


def _forward(x, positions, attn_norm, w_in, hg_lower_bounds, hg_out_norm,
             mla_q_a_norm, mla_w_qb, mla_kv_a_norm, mla_w_kvb, mla_q_norm, mla_k_norm,
             rw_mu, rw_w0, rw_w2, rw_a0, rw_a2, rw_g2, rw_v0, rw_v1, rw_v2,
             rw_k_k, rw_k_a, rw_r_k, rw_ln_w, rw_ln_b,
             w_o, ffn_norm, w_gate_up, w_down, *, cfg):
    batch, seq, d = x.shape
    depth = w_in.shape[0]
    t = batch * seq
    x = x.reshape(t, d)

    lb_sm = jax.nn.softmax(hg_lower_bounds.astype(F32), axis=0)
    lower_bounds = jnp.cumsum(lb_sm, axis=0) - lb_sm[0]
    cos, sin = _rope_tables(positions)

    hg_end = 4 * HG_WIDTH
    mla_end = hg_end + MLA_IN
    w_in_hg = w_in[:, :, :hg_end].astype(BF16)
    w_in_mla = _pad_cols(w_in[:, :, hg_end:mla_end], MLA_IN_PAD).astype(BF16)
    w_in_rw = w_in[:, :, mla_end:].astype(BF16)
    w_qb = _pad_cols(mla_w_qb.reshape(depth, MLA_Q_RANK, MLA_HEADS, MLA_QK_DIM), MLA_QK_PAD)
    w_qb = w_qb.reshape(depth, MLA_Q_RANK, MLA_HEADS * MLA_QK_PAD).astype(BF16)
    w_kvb = mla_w_kvb.astype(BF16)
    split_gain = lambda g: jnp.stack([g[:, :MLA_NOPE], _pad_cols(g[:, MLA_NOPE:], LANES)], axis=1)
    g_q = split_gain(mla_q_norm)
    g_k = split_gain(mla_k_norm)
    w_o_hg = w_o[:, :HG_WIDTH].astype(BF16)
    w_o_mla = w_o[:, HG_WIDTH:HG_WIDTH + MLA_WIDTH].astype(BF16)
    w_o_rw = w_o[:, HG_WIDTH + MLA_WIDTH:].astype(BF16)
    w_gu = w_gate_up.astype(BF16)
    w_dn = w_down.astype(BF16)
    rw_w2b, rw_a2b, rw_g2b = rw_w2.astype(BF16), rw_a2.astype(BF16), rw_g2.astype(BF16)
    rw_v1b, rw_v2b = rw_v1.astype(BF16), rw_v2.astype(BF16)
    r_k = rw_r_k.reshape(depth, RW_WIDTH)

    v_first = None
    for l in range(depth):
        p_hg = norm_matmul(x, attn_norm[l], w_in_hg[l], bm=cfg["proj_bm"], bn=cfg["proj_bn_hg"])
        p_mla = norm_matmul(x, attn_norm[l], w_in_mla[l], bm=cfg["proj_bm"], bn=MLA_IN_PAD)
        p_rw = norm_matmul(x, attn_norm[l], w_in_rw[l], bm=cfg["proj_bm"], bn=cfg["proj_bn_rw"])

        o_hg = hgrn2(p_hg, lower_bounds[l], hg_out_norm[l], batch=batch, bs=cfg["hg_bs"])

        q, k, v = mla_prep(p_mla, cos, sin, mla_q_a_norm[l], w_qb[l], mla_kv_a_norm[l], w_kvb[l], g_q[l], g_k[l],
                           bm=cfg["mla_bm"])
        o_mla = flash_attn(q, k, v, batch=batch, blk=cfg["attn_blk"])

        vres = None if l == 0 else (rw_v0[l - 1], rw_v1b[l - 1], rw_v2b[l - 1], v_first)
        outs = rw_chunk(p_rw, rw_mu[l], rw_w0[l], rw_w2b[l], rw_a0[l], rw_a2b[l], rw_g2b[l],
                        rw_k_k[l], rw_k_a[l], r_k[l], vres, batch=batch, bt=cfg["rw_bt"], prec=cfg["rw_prec"])
        if l == 0:
            v_first = outs[6]
        o_rw = rw_scan(*outs[:6], rw_ln_w[l], rw_ln_b[l], batch=batch, bs=cfg["rw_bs"], prec=cfg["rw_scan_prec"])

        x = out_proj(x, o_hg, o_mla, o_rw, w_o_hg[l], w_o_mla[l], w_o_rw[l], bm=cfg["out_bm"], bn=cfg["out_bn"])
        x = ffn(x, ffn_norm[l], w_gu[l], w_dn[l], bm=cfg["ffn_bm"], bf=cfg["ffn_bf"])
    return x.reshape(batch, seq, d)


_CFG = dict(proj_bm=512, proj_bn_hg=512, proj_bn_rw=896, hg_bs=512, mla_bm=256, attn_blk=1024,
            rw_bt=256, rw_bs=512, rw_prec=HIGHEST, rw_scan_prec=HIGHEST,
            out_bm=512, out_bn=512, ffn_bm=512, ffn_bf=512)


def kernel(x, positions, attn_norm, w_in, hg_lower_bounds, hg_out_norm, mla_q_a_norm, mla_w_qb, mla_kv_a_norm,
           mla_w_kvb, mla_q_norm, mla_k_norm, rw_mu, rw_w0, rw_w2, rw_a0, rw_a2, rw_g2, rw_v0, rw_v1, rw_v2,
           rw_k_k, rw_k_a, rw_r_k, rw_ln_w, rw_ln_b, w_o, ffn_norm, w_gate_up, w_down):
    return _forward(x, positions, attn_norm, w_in, hg_lower_bounds, hg_out_norm, mla_q_a_norm, mla_w_qb,
                    mla_kv_a_norm, mla_w_kvb, mla_q_norm, mla_k_norm, rw_mu, rw_w0, rw_w2, rw_a0, rw_a2, rw_g2,
                    rw_v0, rw_v1, rw_v2, rw_k_k, rw_k_a, rw_r_k, rw_ln_w, rw_ln_b, w_o, ffn_norm, w_gate_up,
                    w_down, cfg=_CFG)
```

```python
import functools

import jax
import jax.numpy as jnp
from jax import lax
from jax.experimental import pallas as pl
from jax.experimental.pallas import tpu as pltpu

F32 = jnp.float32
BF16 = jnp.bfloat16

D_MODEL = 2048
DEPTH = 4
HG_WIDTH = 512
HG_HEAD_DIM = 128
HG_HEADS = HG_WIDTH // HG_HEAD_DIM
MLA_NOPE = 128
MLA_ROPE = 64
MLA_V = 128
MLA_HEADS = 8
MLA_WIDTH = MLA_HEADS * MLA_V
MLA_Q_RANK = 512
MLA_KV_RANK = 256
MLA_QK_DIM = MLA_NOPE + MLA_ROPE
MLA_QK_PAD = 256
MLA_IN = MLA_Q_RANK + MLA_KV_RANK + MLA_ROPE
MLA_IN_PAD = 896
ROPE_THETA = 10000.0
MASK_VALUE = -1e30
RW_WIDTH = 512
RW_HEAD = 64
RW_HEADS = RW_WIDTH // RW_HEAD
RW_PAIRS = RW_HEADS // 2
RW_W_RANK = 64
RW_A_RANK = 64
RW_V_RANK = 32
RW_G_RANK = 128
RW_IN = 3 * RW_WIDTH + RW_W_RANK + RW_A_RANK + RW_G_RANK
RW_GN_EPS = 64e-5
MIX_WIDTH = HG_WIDTH + MLA_WIDTH + RW_WIDTH
D_FF = 5632
RMS_EPS = 1e-6

HG_CHUNK = 32
HG_EXP_CLAMP = 80.0
HG_CHUNKS_PER_STEP = 4
RW_CHUNK = 64
RW_CHUNKS_PER_STEP = 4
ATTN_ROW_CHAINS = 4
LANES = 128

_MIB = 1024 * 1024


def _cparams(semantics, vmem_mib):
    return pltpu.CompilerParams(dimension_semantics=semantics, vmem_limit_bytes=vmem_mib * _MIB)


def _dot(a, b):
    return jnp.dot(a, b, preferred_element_type=F32)


def _dot_nt(a, b):
    return lax.dot_general(a, b, (((1,), (1,)), ((), ())), preferred_element_type=F32)


def _dot_tn(a, b):
    return lax.dot_general(a, b, (((0,), (0,)), ((), ())), preferred_element_type=F32)


def _split(x, terms):
    parts = []
    for _ in range(terms - 1):
        hi = x.astype(BF16)
        parts.append(hi)
        x = x - hi.astype(F32)
    parts.append(x.astype(BF16))
    return parts


def _dot_sel_lhs(sel, x, terms=3):
    return sum(_dot(sel, part) for part in _split(x, terms))


def _dot_sel_rhs(x, sel, terms=3):
    return sum(_dot(part, sel) for part in _split(x, terms))


def _rms(x, width):
    ms = jnp.sum(x * x, axis=-1, keepdims=True) * (1.0 / width)
    return x * lax.rsqrt(ms + RMS_EPS)


def _in_proj_body(x_ref, g_ref, whg_ref, wmla_ref, wrw_ref, ohg_ref, omla_ref, orw_ref):
    h = (_rms(x_ref[...], x_ref.shape[-1]) * g_ref[...]).astype(BF16)
    ohg_ref[...] = _dot(h, whg_ref[...])
    omla_ref[...] = _dot(h, wmla_ref[...])
    orw_ref[...] = _dot(h, wrw_ref[...])


def in_proj(x, gain, w_hg, w_mla, w_rw, *, bm):
    t, d = x.shape
    resident = lambda a: pl.BlockSpec(a.shape, lambda i: (0, 0), pipeline_mode=pl.Buffered(1))
    rows = lambda width: pl.BlockSpec((bm, width), lambda i: (i, 0))
    widths = [w_hg.shape[1], w_mla.shape[1], w_rw.shape[1]]
    return pl.pallas_call(
        _in_proj_body,
        grid=(t // bm,),
        in_specs=[rows(d), pl.BlockSpec((1, d), lambda i: (0, 0)), resident(w_hg), resident(w_mla), resident(w_rw)],
        out_specs=[rows(n) for n in widths],
        out_shape=[jax.ShapeDtypeStruct((t, n), F32) for n in widths],
        compiler_params=_cparams(("parallel",), 56),
        name="in_proj",
    )(x, gain.reshape(1, d), w_hg, w_mla, w_rw)


def _out_proj_body(x_ref, a_ref, b_ref, c_ref, wa_ref, wb_ref, wc_ref, o_ref):
    acc = _dot(a_ref[...], wa_ref[...]) + _dot(b_ref[...], wb_ref[...]) + _dot(c_ref[...], wc_ref[...])
    o_ref[...] = x_ref[...] + acc


def out_proj(x, o_hg, o_mla, o_rw, w_hg, w_mla, w_rw, *, bm):
    t, d = x.shape
    row = lambda width: pl.BlockSpec((bm, width), lambda i: (i, 0))
    resident = lambda a: pl.BlockSpec(a.shape, lambda i: (0, 0), pipeline_mode=pl.Buffered(1))
    return pl.pallas_call(
        _out_proj_body,
        grid=(t // bm,),
        in_specs=[row(d), row(HG_WIDTH), row(MLA_WIDTH), row(RW_WIDTH),
                  resident(w_hg), resident(w_mla), resident(w_rw)],
        out_specs=row(d),
        out_shape=jax.ShapeDtypeStruct((t, d), F32),
        compiler_params=_cparams(("parallel",), 48),
        name="out_proj",
    )(x, o_hg, o_mla, o_rw, w_hg, w_mla, w_rw)


def _ffn_body(x_ref, g_ref, wg_ref, wu_ref, wd_ref, o_ref, h_ref):
    @pl.when(pl.program_id(1) == 0)
    def _():
        x = x_ref[...]
        h_ref[...] = (_rms(x, x.shape[-1]) * g_ref[...]).astype(BF16)
        o_ref[...] = x

    h = h_ref[...]
    gate = _dot(h, wg_ref[...])
    up = _dot(h, wu_ref[...])
    act = (gate * jax.nn.sigmoid(gate) * up).astype(BF16)
    o_ref[...] += _dot(act, wd_ref[...])


def ffn(x, gain, w_gate_up, w_down, *, bm, bf):
    t, d = x.shape
    dff = w_down.shape[0]
    nf = dff // bf
    return pl.pallas_call(
        _ffn_body,
        grid=(t // bm, nf),
        in_specs=[
            pl.BlockSpec((bm, d), lambda i, j: (i, 0)),
            pl.BlockSpec((1, d), lambda i, j: (0, 0)),
            pl.BlockSpec((d, bf), lambda i, j: (0, j)),
            pl.BlockSpec((d, bf), lambda i, j: (0, j + nf)),
            pl.BlockSpec((bf, d), lambda i, j: (j, 0)),
        ],
        out_specs=pl.BlockSpec((bm, d), lambda i, j: (i, 0)),
        out_shape=jax.ShapeDtypeStruct((t, d), F32),
        scratch_shapes=[pltpu.VMEM((bm, d), BF16)],
        compiler_params=_cparams(("parallel", "arbitrary"), 56),
        name="ffn",
    )(x, gain.reshape(1, d), w_gate_up, w_gate_up, w_down)


def _hgrn2_body(p_ref, lb_ref, gain_ref, o_ref, st_ref):
    c = HG_CHUNK
    w = HG_WIDTH

    @pl.when(pl.program_id(1) == 0)
    def _():
        st_ref[...] = jnp.zeros_like(st_ref)

    row = lax.broadcasted_iota(jnp.int32, (c, c), 0)
    col = lax.broadcasted_iota(jnp.int32, (c, c), 1)
    tril = col <= row
    cum = tril.astype(BF16)
    lb = lb_ref[...]
    gain = gain_ref[...]

    heads = [slice(h * HG_HEAD_DIM, (h + 1) * HG_HEAD_DIM) for h in range(HG_HEADS)]

    def chunk_stages(rows, states, u):
        q_raw = p_ref[rows, 0:w]
        z = p_ref[rows, w:2 * w]
        v = p_ref[rows, 2 * w:3 * w]
        g_raw = p_ref[rows, 3 * w:4 * w]
        q = q_raw * jax.nn.sigmoid(q_raw)
        log_f = jnp.log(lb + (1.0 - lb) * jax.nn.sigmoid(z))
        k = (1.0 - lb) * jax.nn.sigmoid(-z)
        b = _dot_sel_lhs(cum, log_f)
        yield
        b_mid = b[c // 2 - 1:c // 2, :]
        b_end = b[c - 1:c, :]
        q_in = (q * jnp.exp(jnp.minimum(b - b_mid, HG_EXP_CLAMP))).astype(BF16)
        k_in = (k * jnp.exp(jnp.minimum(b_mid - b, HG_EXP_CLAMP))).astype(BF16)
        q_st = (q * jnp.exp(b)).astype(BF16)
        k_st = (k * jnp.exp(b_end - b)).astype(BF16)
        d_end = jnp.exp(b_end)
        vb = v.astype(BF16)
        gate = g_raw * jax.nn.sigmoid(g_raw)
        scores = [jnp.where(tril, _dot_nt(q_in[:, sl], k_in[:, sl]), 0.0).astype(BF16) for sl in heads]
        updates = [_dot_tn(vb[:, sl], k_st[:, sl]) for sl in heads]
        yield
        st = states[u]
        states[u + 1] = [st[h] * d_end[:, sl] + updates[h] for h, sl in enumerate(heads)]
        outs = [_dot_nt(q_st[:, sl], st[h].astype(BF16)) + _dot(scores[h], vb[:, sl]) for h, sl in enumerate(heads)]
        yield
        for h, sl in enumerate(heads):
            o_ref[rows, sl] = (_rms(outs[h], HG_HEAD_DIM) * gain * gate[:, sl]).astype(o_ref.dtype)

    def step(si, carry):
        states = [[st_ref[h] for h in range(HG_HEADS)]] + [None] * HG_CHUNKS_PER_STEP
        base = si * HG_CHUNKS_PER_STEP
        _lock_step([chunk_stages(pl.ds(pl.multiple_of((base + u) * c, c), c), states, u)
                    for u in range(HG_CHUNKS_PER_STEP)])
        for h in range(HG_HEADS):
            st_ref[h] = states[HG_CHUNKS_PER_STEP][h]
        return carry

    lax.fori_loop(0, p_ref.shape[0] // (c * HG_CHUNKS_PER_STEP), step, 0)


def hgrn2(p_hg, lower_bound, out_gain, *, batch, bs):
    t = p_hg.shape[0]
    ns = t // batch // bs
    assert t == batch * ns * bs and bs % (HG_CHUNK * HG_CHUNKS_PER_STEP) == 0
    return pl.pallas_call(
        _hgrn2_body,
        grid=(batch, ns),
        in_specs=[
            pl.BlockSpec((bs, 4 * HG_WIDTH), lambda b, s: (b * ns + s, 0)),
            pl.BlockSpec((1, HG_WIDTH), lambda b, s: (0, 0)),
            pl.BlockSpec((1, HG_HEAD_DIM), lambda b, s: (0, 0)),
        ],
        out_specs=pl.BlockSpec((bs, HG_WIDTH), lambda b, s: (b * ns + s, 0)),
        out_shape=jax.ShapeDtypeStruct((t, HG_WIDTH), BF16),
        scratch_shapes=[pltpu.VMEM((HG_HEADS, HG_HEAD_DIM, HG_HEAD_DIM), F32)],
        compiler_params=_cparams(("parallel", "arbitrary"), 32),
        name="hgrn2",
    )(p_hg, lower_bound.reshape(1, HG_WIDTH), out_gain.reshape(1, HG_HEAD_DIM))


def _rope(x, cos, sin):
    half = MLA_ROPE // 2
    lane = lax.broadcasted_iota(jnp.int32, x.shape, 1)
    rot = jnp.where(lane < half, -pltpu.roll(x, LANES - half, 1), pltpu.roll(x, half, 1))
    return x * cos + rot * sin


def _mla_prep_body(p_ref, cos_ref, sin_ref, gqa_ref, wqb_ref, gkva_ref, wkvb_ref, gq_ref, gk_ref,
                   q_ref, k_ref, v_ref):
    p = p_ref[...]
    cos = cos_ref[...]
    sin = sin_ref[...]
    scale = MLA_QK_DIM ** -0.5
    qn = (_rms(p[:, 0:MLA_Q_RANK], MLA_Q_RANK) * gqa_ref[...]).astype(BF16)
    q = _dot(qn, wqb_ref[...])
    kvn = (_rms(p[:, MLA_Q_RANK:MLA_Q_RANK + MLA_KV_RANK], MLA_KV_RANK) * gkva_ref[...]).astype(BF16)
    kv = _dot(kvn, wkvb_ref[...])
    gq_nope, gq_rope = gq_ref[0:1, :], gq_ref[1:2, :]
    gk_nope, gk_rope = gk_ref[0:1, :], gk_ref[1:2, :]
    k_rope = _rope(_rms(p[:, MLA_Q_RANK + MLA_KV_RANK:], MLA_ROPE) * gk_rope, cos, sin).astype(BF16)
    for h in range(MLA_HEADS):
        o = h * MLA_QK_PAD
        q_nope = _rms(q[:, o:o + MLA_NOPE], MLA_NOPE) * (gq_nope * scale)
        q_rope = _rope(_rms(q[:, o + MLA_NOPE:o + MLA_QK_PAD], MLA_ROPE) * (gq_rope * scale), cos, sin)
        q_ref[h, :, 0:MLA_NOPE] = q_nope.astype(BF16)
        q_ref[h, :, MLA_NOPE:MLA_QK_PAD] = q_rope.astype(BF16)
        k_ref[h, :, 0:MLA_NOPE] = (_rms(kv[:, o:o + MLA_NOPE], MLA_NOPE) * gk_nope).astype(BF16)
        k_ref[h, :, MLA_NOPE:MLA_QK_PAD] = k_rope
        v_ref[h] = kv[:, o + MLA_NOPE:o + MLA_QK_PAD].astype(BF16)


def mla_prep(p_mla, cos, sin, g_qa, w_qb, g_kva, w_kvb, g_q, g_k, *, bm):
    t = p_mla.shape[0]
    full = lambda a: pl.BlockSpec(a.shape, lambda i: (0,) * a.ndim)
    rows = lambda width: pl.BlockSpec((bm, width), lambda i: (i, 0))
    heads = lambda width: pl.BlockSpec((MLA_HEADS, bm, width), lambda i: (0, i, 0))
    g_qa = g_qa.reshape(1, -1)
    g_kva = g_kva.reshape(1, -1)
    return pl.pallas_call(
        _mla_prep_body,
        grid=(t // bm,),
        in_specs=[rows(MLA_IN_PAD), rows(LANES), rows(LANES), full(g_qa), full(w_qb), full(g_kva), full(w_kvb),
                  full(g_q), full(g_k)],
        out_specs=[heads(MLA_QK_PAD), heads(MLA_QK_PAD), heads(MLA_V)],
        out_shape=[jax.ShapeDtypeStruct((MLA_HEADS, t, MLA_QK_PAD), BF16),
                   jax.ShapeDtypeStruct((MLA_HEADS, t, MLA_QK_PAD), BF16),
                   jax.ShapeDtypeStruct((MLA_HEADS, t, MLA_V), BF16)],
        compiler_params=_cparams(("parallel",), 48),
        name="mla_prep",
    )(p_mla, cos, sin, g_qa, w_qb, g_kva, w_kvb, g_q, g_k)


def _pipelined(stages):
    n = len(stages)
    done = [False] * n
    rnd = 0
    while not all(done):
        for i in range(min(rnd, n - 1) + 1):
            if not done[i]:
                try:
                    next(stages[i])
                except StopIteration:
                    done[i] = True
        rnd += 1


def _flash_body(i_tab, j_tab, q_ref, k_ref, v_ref, o_ref, m_ref, l_ref, acc_ref):
    pair = pl.program_id(2)
    i = i_tab[pair]
    j = j_tab[pair]
    blk = q_ref.shape[1]
    rb = blk // ATTN_ROW_CHAINS

    @pl.when(j == 0)
    def _():
        m_ref[...] = jnp.full_like(m_ref, MASK_VALUE)
        l_ref[...] = jnp.zeros_like(l_ref)
        acc_ref[...] = jnp.zeros_like(acc_ref)

    def row_group(r, diagonal):
        rows = slice(r * rb, (r + 1) * rb)
        ncol = (r + 1) * rb if diagonal else blk
        s = _dot_nt(q_ref[0, rows, :], k_ref[0, 0:ncol, :])
        yield
        if diagonal:
            row = lax.broadcasted_iota(jnp.int32, (rb, rb), 0)
            col = lax.broadcasted_iota(jnp.int32, (rb, rb), 1)
            last = jnp.where(col <= row, s[:, r * rb:], MASK_VALUE)
            s = jnp.concatenate([s[:, :r * rb], last], axis=1) if r else last
        m_old = m_ref[rows, :]
        m_new = jnp.maximum(m_old, jnp.max(s, axis=-1, keepdims=True))
        alpha = jnp.exp(m_old - m_new)
        p = jnp.exp(s - jnp.concatenate([m_new] * (ncol // LANES), axis=1))
        l_new = alpha * l_ref[rows, :] + jnp.sum(p, axis=-1, keepdims=True)
        pb = p.astype(BF16)
        yield
        acc = alpha * acc_ref[rows, :] + _dot(pb, v_ref[0, 0:ncol, :])
        if diagonal:
            o_ref[rows, :] = (acc / l_new).astype(o_ref.dtype)
        else:
            m_ref[rows, :] = m_new
            l_ref[rows, :] = l_new
            acc_ref[rows, :] = acc

    @pl.when(j < i)
    def _():
        _pipelined([row_group(r, False) for r in range(ATTN_ROW_CHAINS)])

    @pl.when(j == i)
    def _():
        _pipelined([row_group(r, True) for r in range(ATTN_ROW_CHAINS)])


def flash_attn(q, k, v, *, batch, blk):
    heads, t, _ = q.shape
    nb = t // batch // blk
    assert t == batch * nb * blk and blk % (ATTN_ROW_CHAINS * LANES) == 0
    pairs =[(i, j) for i in range(nb) for j in range(i + 1)]
    i_tab = jnp.array([p[0] for p in pairs], jnp.int32)
    j_tab = jnp.array([p[1] for p in pairs], jnp.int32)
    qmap = lambda b, h, p, it, jt: (h, b * nb + it[p], 0)
    kmap = lambda b, h, p, it, jt: (h, b * nb + jt[p], 0)
    return pl.pallas_call(
        _flash_body,
        grid_spec=pltpu.PrefetchScalarGridSpec(
            num_scalar_prefetch=2,
            grid=(batch, heads, len(pairs)),
            in_specs=[pl.BlockSpec((1, blk, MLA_QK_PAD), qmap),
                      pl.BlockSpec((1, blk, MLA_QK_PAD), kmap),
                      pl.BlockSpec((1, blk, MLA_V), kmap)],
            out_specs=pl.BlockSpec((blk, MLA_V), lambda b, h, p, it, jt: (b * nb + it[p], h)),
            scratch_shapes=[pltpu.VMEM((blk, LANES), F32), pltpu.VMEM((blk, LANES), F32),
                            pltpu.VMEM((blk, MLA_V), F32)],
        ),
        out_shape=jax.ShapeDtypeStruct((t, heads * MLA_V), BF16),
        compiler_params=_cparams(("parallel", "parallel", "arbitrary"), 48),
        name="flash_attn",
    )(i_tab, j_tab, q, k, v)


def _head_sums(x, weight):
    half = 2 * LANES
    gi = lax.broadcasted_iota(jnp.int32, (half, half), 0) // RW_HEAD
    gj = lax.broadcasted_iota(jnp.int32, (half, half), 1) // RW_HEAD
    sel = jnp.where(gi == gj, weight, 0.0).astype(BF16)
    return jnp.concatenate([_dot_sel_rhs(x[:, 0:half], sel, 2), _dot_sel_rhs(x[:, half:], sel, 2)], axis=1)


def _lock_step(stages):
    results = [None] * len(stages)
    live = list(range(len(stages)))
    while live:
        for i in list(live):
            try:
                next(stages[i])
            except StopIteration as stop:
                results[i] = stop.value
                live.remove(i)
    return results


def _unit_lower_inverse(a, block):
    n = a.shape[0]
    row = lax.broadcasted_iota(jnp.int32, (n, n), 0)
    col = lax.broadcasted_iota(jnp.int32, (n, n), 1)
    diff = row ^ col
    t = jnp.where(row == col, 1.0, 0.0) + jnp.where(diff < 2, a, 0.0)
    size = 2
    while size < block:
        off = jnp.where((diff >= size) & (diff < 2 * size), a, 0.0).astype(BF16)
        tb = t.astype(BF16)
        half = _dot(tb, off).astype(BF16)
        yield
        t = t + _dot(half, tb)
        yield
        size *= 2
    return t


def _rw_chunk_body(has_vres, *refs):
    if has_vres:
        (p_ref, prev_ref, mu_ref, w0_ref, w2_ref, a0_ref, a2_ref, g2_ref, kk_ref, ka_ref, rk_ref,
         v0_ref, v1_ref, v2_ref, vf_ref,
         m_ref, nt_ref, dec_ref, qh_ref, oh_ref, gate_ref, bonus_ref,
         r_s, g_s, k_s, v_s, kk_s, b_s) = refs
    else:
        (p_ref, prev_ref, mu_ref, w0_ref, w2_ref, a0_ref, a2_ref, g2_ref, kk_ref, ka_ref, rk_ref,
         m_ref, nt_ref, dec_ref, qh_ref, oh_ref, gate_ref, bonus_ref, vout_ref,
         r_s, g_s, k_s, v_s, kk_s, b_s) = refs
    c = RW_CHUNK
    w = RW_WIDTH
    bt = p_ref.shape[0]

    p = p_ref[...]
    first = pl.program_id(1) == 0
    prev_row = jnp.where(first, 0.0, prev_ref[7:8, :])
    rowi = lax.broadcasted_iota(jnp.int32, p.shape, 0)
    prev = jnp.where(rowi == 0, prev_row, pltpu.roll(p, 1, 0))
    p = p + (prev - p) * mu_ref[...]
    r = p[:, 0:w]
    k = p[:, w:2 * w]
    v = p[:, 2 * w:3 * w]
    wd = p[:, 3 * w:3 * w + RW_W_RANK]
    ad = p[:, 3 * w + RW_W_RANK:3 * w + RW_W_RANK + RW_A_RANK]
    gd = p[:, 3 * w + RW_W_RANK + RW_A_RANK:]
    w_log = -jax.nn.softplus(-(w0_ref[...] + _dot(jnp.tanh(wd).astype(BF16), w2_ref[...]))) - 0.5
    lr = jax.nn.sigmoid(a0_ref[...] + _dot(ad.astype(BF16), a2_ref[...]))
    gate = _dot(jax.nn.sigmoid(gd).astype(BF16), g2_ref[...])
    if has_vres:
        mix = jax.nn.sigmoid(v0_ref[...] + _dot(_dot(v.astype(BF16), v1_ref[...]).astype(BF16), v2_ref[...]))
        v = v + (vf_ref[...] - v) * mix
    else:
        vout_ref[...] = v
    kk = k * kk_ref[...]
    kk = kk / jnp.maximum(jnp.sqrt(_head_sums(kk * kk, 1.0)), 1e-12)
    k = k * (1.0 + (lr - 1.0) * ka_ref[...])
    gate_ref[...] = gate
    bonus_ref[...] = _head_sums(r * k * rk_ref[...], 1.0) * v
    r_s[...] = r
    g_s[...] = -jnp.exp(w_log)
    k_s[...] = k
    v_s[...] = v
    kk_s[...] = kk
    b_s[...] = kk * lr

    n2 = 2 * c
    row = lax.broadcasted_iota(jnp.int32, (n2, n2), 0)
    col = lax.broadcasted_iota(jnp.int32, (n2, n2), 1)
    same_head = (row ^ col) < c
    incl = same_head & (col <= row)
    strict = same_head & (col < row)
    ci_row = lax.broadcasted_iota(jnp.int32, (c, c), 0)
    ci_col = lax.broadcasted_iota(jnp.int32, (c, c), 1)
    cum = (ci_col <= ci_row).astype(BF16)
    lane = lax.broadcasted_iota(jnp.int32, (1, LANES), 1)
    head0 = lane < RW_HEAD

    def stack(z):
        return jnp.concatenate([jnp.where(head0, z, 0.0), jnp.where(head0, 0.0, z)], axis=0)

    def pair_matrices(g, r, k, v, kkc, bc):
        gc = _dot_sel_lhs(cum, g)
        yield
        g_end = gc[c - 1:c, :]
        inv = jnp.exp(-gc)
        to_end = jnp.exp(g_end - gc)
        a2 = stack(-kkc * jnp.exp(gc - g)).astype(BF16)
        r2 = stack(r * jnp.exp(gc))
        v2 = stack(v).astype(BF16)
        b_end = stack(bc * to_end).astype(BF16)
        k_end = stack(k * to_end).astype(BF16)
        b_inv = (bc * inv).astype(BF16)
        k_inv = (k * inv).astype(BF16)
        ar = jnp.concatenate([a2, r2.astype(BF16)], axis=0)
        ab = _dot_nt(ar, jnp.concatenate([b_inv, b_inv], axis=0))
        ak = _dot_nt(ar, jnp.concatenate([k_inv, k_inv], axis=0))
        yield
        a_ab = jnp.where(strict, ab[0:n2], 0.0)
        a_rb = jnp.where(incl, ab[n2:2 * n2], 0.0).astype(BF16)
        a_ak = jnp.where(strict, ak[0:n2], 0.0).astype(BF16)
        a_rk = jnp.where(incl, ak[n2:2 * n2], 0.0).astype(BF16)
        x = _dot(a_ak, v2).astype(BF16)
        nt = _dot_tn(v2, k_end)
        oh2 = _dot(a_rk, v2)
        t_inv = (yield from _unit_lower_inverse(a_ab, c)).astype(BF16)
        p1 = _dot(t_inv, a2).astype(BF16)
        p2 = _dot(t_inv, x).astype(BF16)
        yield
        qh2 = r2 + _dot(a_rb, p1)
        oh2 = oh2 + _dot(a_rb, p2)
        m = _dot_tn(b_end, p1)
        nt = nt + _dot_tn(p2, b_end)
        return m, nt, jnp.exp(g_end), qh2[0:c] + qh2[c:n2], oh2[0:c] + oh2[c:n2]

    def chunk(ci, carry):
        units = [(ci * RW_CHUNKS_PER_STEP + u, pr, slice(pr * LANES, (pr + 1) * LANES))
                 for u in range(RW_CHUNKS_PER_STEP) for pr in range(RW_PAIRS)]
        rows = lambda cj: pl.ds(pl.multiple_of(cj * c, c), c)
        loaded = [[s[rows(cj), sl] for s in (g_s, r_s, k_s, v_s, kk_s, b_s)] for cj, _, sl in units]
        results = _lock_step([pair_matrices(*operands) for operands in loaded])
        for (cj, pr, sl), (m, nt, dec, qh, oh) in zip(units, results):
            m_ref[cj, pr] = m.astype(m_ref.dtype)
            nt_ref[cj, pr] = nt
            dec_ref[cj, pr] = dec
            qh_ref[rows(cj), sl] = qh.astype(qh_ref.dtype)
            oh_ref[rows(cj), sl] = oh
        return carry

    lax.fori_loop(0, bt // c // RW_CHUNKS_PER_STEP, chunk, 0)


def rw_chunk(p_rw, mu, w0, w2, a0, a2, g2, k_k, k_a, r_k, vres, *, batch, bt):
    t = p_rw.shape[0]
    nb = t // batch // bt
    nc = bt // RW_CHUNK
    assert t == batch * nb * bt and bt % (RW_CHUNK * RW_CHUNKS_PER_STEP) == 0
    has_vres = vres is not None
    row1 = lambda a: a.reshape(1, -1)
    full = lambda a: pl.BlockSpec(a.shape, lambda b, s: (0,) * a.ndim)
    rows = lambda width: pl.BlockSpec((bt, width), lambda b, s: (b * nb + s, 0))
    prev_spec = pl.BlockSpec((8, RW_IN), lambda b, s: (jnp.maximum((b * nb + s) * (bt // 8) - 1, 0), 0))
    mats = pl.BlockSpec((nc, RW_PAIRS, LANES, LANES), lambda b, s: (b * nb + s, 0, 0, 0))
    params = [row1(mu), row1(w0), w2, row1(a0), a2, g2, row1(k_k), row1(k_a), row1(r_k)]
    args = [p_rw, p_rw] + params
    in_specs = [rows(RW_IN), prev_spec] + [full(a) for a in params]
    if has_vres:
        v0, v1, v2, v_first = vres
        extra = [row1(v0), v1, v2]
        args += extra + [v_first]
        in_specs += [full(a) for a in extra] + [rows(RW_WIDTH)]
    decs = pl.BlockSpec((nc, RW_PAIRS, 1, LANES), lambda b, s: (b * nb + s, 0, 0, 0))
    mat_shape = lambda dt: jax.ShapeDtypeStruct((t // RW_CHUNK, RW_PAIRS, LANES, LANES), dt)
    dec_shape = jax.ShapeDtypeStruct((t // RW_CHUNK, RW_PAIRS, 1, LANES), F32)
    tok_shape = lambda dt: jax.ShapeDtypeStruct((t, RW_WIDTH), dt)
    out_specs = [mats, mats, decs, rows(RW_WIDTH), rows(RW_WIDTH), rows(RW_WIDTH), rows(RW_WIDTH)]
    out_shape = [mat_shape(BF16), mat_shape(F32), dec_shape, tok_shape(BF16), tok_shape(F32), tok_shape(F32),
                 tok_shape(F32)]
    if not has_vres:
        out_specs.append(rows(RW_WIDTH))
        out_shape.append(tok_shape(F32))
    return pl.pallas_call(
        functools.partial(_rw_chunk_body, has_vres),
        grid=(batch, nb),
        in_specs=in_specs,
        out_specs=out_specs,
        out_shape=out_shape,
        scratch_shapes=[pltpu.VMEM((bt, RW_WIDTH), F32) for _ in range(6)],
        compiler_params=_cparams(("parallel", "arbitrary"), 48),
        name="rw_chunk",
    )(*args)


def _rw_scan_body(m_ref, nt_ref, dec_ref, qh_ref, oh_ref, gate_ref, bonus_ref, lnw_ref, lnb_ref, o_ref, y_ref, h_ref):
    c = RW_CHUNK

    @pl.when(pl.program_id(1) == 0)
    def _():
        h_ref[...] = jnp.zeros_like(h_ref)

    def chunk(ci, carry):
        rows = pl.ds(pl.multiple_of(ci * c, c), c)
        for pr in range(RW_PAIRS):
            sl = slice(pr * LANES, (pr + 1) * LANES)
            ht = h_ref[pr]
            hb = ht.astype(BF16)
            y_ref[rows, sl] = _dot_nt(qh_ref[rows, sl], hb) + oh_ref[rows, sl]
            h_ref[pr] = ht * dec_ref[ci, pr] + _dot_nt(hb, m_ref[ci, pr]) + nt_ref[ci, pr]
        return carry

    lax.fori_loop(0, qh_ref.shape[0] // c, chunk, 0)

    y = y_ref[...]
    mean = _head_sums(y, 1.0 / RW_HEAD)
    d = y - mean
    var = _head_sums(d * d, 1.0 / RW_HEAD)
    y = d * lax.rsqrt(var + RW_GN_EPS) * lnw_ref[...] + lnb_ref[...]
    o_ref[...] = ((y + bonus_ref[...]) * gate_ref[...]).astype(o_ref.dtype)


def rw_scan(m, nt, dec, qh, oh, gate, bonus, ln_w, ln_b, *, batch, bs):
    t = qh.shape[0]
    nb = t // batch // bs
    nc = bs // RW_CHUNK
    rows = pl.BlockSpec((bs, RW_WIDTH), lambda b, s: (b * nb + s, 0))
    mats = pl.BlockSpec((nc, RW_PAIRS, LANES, LANES), lambda b, s: (b * nb + s, 0, 0, 0))
    decs = pl.BlockSpec((nc, RW_PAIRS, 1, LANES), lambda b, s: (b * nb + s, 0, 0, 0))
    vec = pl.BlockSpec((1, RW_WIDTH), lambda b, s: (0, 0))
    return pl.pallas_call(
        _rw_scan_body,
        grid=(batch, nb),
        in_specs=[mats, mats, decs, rows, rows, rows, rows, vec, vec],
        out_specs=rows,
        out_shape=jax.ShapeDtypeStruct((t, RW_WIDTH), BF16),
        scratch_shapes=[pltpu.VMEM((bs, RW_WIDTH), F32), pltpu.VMEM((RW_PAIRS, LANES, LANES), F32)],
        compiler_params=_cparams(("parallel", "arbitrary"), 48),
        name="rw_scan",
    )(m, nt, dec, qh, oh, gate, bonus, ln_w.reshape(1, -1), ln_b.reshape(1, -1))


def _rope_tables(positions):
    half = MLA_ROPE // 2
    inv_freq = ROPE_THETA ** (-jnp.arange(half, dtype=F32) / half)
    ang = positions.astype(F32).reshape(-1, 1) * inv_freq
    zeros = jnp.zeros((ang.shape[0], LANES - MLA_ROPE), F32)
    cos = jnp.concatenate([jnp.cos(ang), jnp.cos(ang), zeros], axis=1)
    sin = jnp.concatenate([jnp.sin(ang), jnp.sin(ang), zeros], axis=1)
    return cos, sin


def _pad_cols(a, width):
    return jnp.pad(a, [(0, 0)] * (a.ndim - 1) + [(0, width - a.shape[-1])])


def _forward(x, positions, attn_norm, w_in, hg_lower_bounds, hg_out_norm,
             mla_q_a_norm, mla_w_qb, mla_kv_a_norm, mla_w_kvb, mla_q_norm, mla_k_norm,
             rw_mu, rw_w0, rw_w2, rw_a0, rw_a2, rw_g2, rw_v0, rw_v1, rw_v2,
             rw_k_k, rw_k_a, rw_r_k, rw_ln_w, rw_ln_b,
             w_o, ffn_norm, w_gate_up, w_down, *, cfg):
    batch, seq, d = x.shape
    depth = w_in.shape[0]
    t = batch * seq
    x = x.reshape(t, d)

    lb_sm = jax.nn.softmax(hg_lower_bounds.astype(F32), axis=0)
    lower_bounds = jnp.cumsum(lb_sm, axis=0) - lb_sm[0]
    cos, sin = _rope_tables(positions)

    hg_end = 4 * HG_WIDTH
    mla_end = hg_end + MLA_IN
    w_in_hg = w_in[:, :, :hg_end].astype(BF16)
    w_in_mla = _pad_cols(w_in[:, :, hg_end:mla_end], MLA_IN_PAD).astype(BF16)
    w_in_rw = w_in[:, :, mla_end:].astype(BF16)
    w_qb = _pad_cols(mla_w_qb.reshape(depth, MLA_Q_RANK, MLA_HEADS, MLA_QK_DIM), MLA_QK_PAD)
    w_qb = w_qb.reshape(depth, MLA_Q_RANK, MLA_HEADS * MLA_QK_PAD).astype(BF16)
    w_kvb = mla_w_kvb.astype(BF16)
    split_gain = lambda g: jnp.stack([g[:, :MLA_NOPE], _pad_cols(g[:, MLA_NOPE:], LANES)], axis=1)
    g_q = split_gain(mla_q_norm)
    g_k = split_gain(mla_k_norm)
    w_o_hg = w_o[:, :HG_WIDTH].astype(BF16)
    w_o_mla = w_o[:, HG_WIDTH:HG_WIDTH + MLA_WIDTH].astype(BF16)
    w_o_rw = w_o[:, HG_WIDTH + MLA_WIDTH:].astype(BF16)
    w_gu = w_gate_up.astype(BF16)
    w_dn = w_down.astype(BF16)
    rw_w2b, rw_a2b, rw_g2b = rw_w2.astype(BF16), rw_a2.astype(BF16), rw_g2.astype(BF16)
    rw_v1b, rw_v2b = rw_v1.astype(BF16), rw_v2.astype(BF16)
    r_k = rw_r_k.reshape(depth, RW_WIDTH)

    v_first = None
    for l in range(depth):
        p_hg, p_mla, p_rw = in_proj(x, attn_norm[l], w_in_hg[l], w_in_mla[l], w_in_rw[l], bm=cfg["proj_bm"])

        o_hg = hgrn2(p_hg, lower_bounds[l], hg_out_norm[l], batch=batch, bs=cfg["hg_bs"])

        q, k, v = mla_prep(p_mla, cos, sin, mla_q_a_norm[l], w_qb[l], mla_kv_a_norm[l], w_kvb[l], g_q[l], g_k[l],
                           bm=cfg["mla_bm"])
        o_mla = flash_attn(q, k, v, batch=batch, blk=cfg["attn_blk"])

        vres = None if l == 0 else (rw_v0[l - 1], rw_v1b[l - 1], rw_v2b[l - 1], v_first)
        outs = rw_chunk(p_rw, rw_mu[l], rw_w0[l], rw_w2b[l], rw_a0[l], rw_a2b[l], rw_g2b[l],
                        rw_k_k[l], rw_k_a[l], r_k[l], vres, batch=batch, bt=cfg["rw_bt"])
        if l == 0:
            v_first = outs[7]
        o_rw = rw_scan(*outs[:7], rw_ln_w[l], rw_ln_b[l], batch=batch, bs=cfg["rw_bs"])

        x = out_proj(x, o_hg, o_mla, o_rw, w_o_hg[l], w_o_mla[l], w_o_rw[l], bm=cfg["out_bm"])
        x = ffn(x, ffn_norm[l], w_gu[l], w_dn[l], bm=cfg["ffn_bm"], bf=cfg["ffn_bf"])
    return x.reshape(batch, seq, d)


_CFG = dict(proj_bm=256, hg_bs=512, mla_bm=256, attn_blk=1024, rw_bt=256, rw_bs=512,
            out_bm=512, ffn_bm=512, ffn_bf=512)


def kernel(x, positions, attn_norm, w_in, hg_lower_bounds, hg_out_norm, mla_q_a_norm, mla_w_qb, mla_kv_a_norm,
           mla_w_kvb, mla_q_norm, mla_k_norm, rw_mu, rw_w0, rw_w2, rw_a0, rw_a2, rw_g2, rw_v0, rw_v1, rw_v2,
           rw_k_k, rw_k_a, rw_r_k, rw_ln_w, rw_ln_b, w_o, ffn_norm, w_gate_up, w_down):
    return _forward(x, positions, attn_norm, w_in, hg_lower_bounds, hg_out_norm, mla_q_a_norm, mla_w_qb,
                    mla_kv_a_norm, mla_w_kvb, mla_q_norm, mla_k_norm, rw_mu, rw_w0, rw_w2, rw_a0, rw_a2, rw_g2,
                    rw_v0, rw_v1, rw_v2, rw_k_k, rw_k_a, rw_r_k, rw_ln_w, rw_ln_b, w_o, ffn_norm, w_gate_up,
                    w_down, cfg=_CFG)
```

```python
import functools

import jax
import jax.numpy as jnp
from jax import lax
from jax.experimental import pallas as pl
from jax.experimental.pallas import tpu as pltpu

F32 = jnp.float32
BF16 = jnp.bfloat16

D_MODEL = 2048
DEPTH = 4
HG_WIDTH = 512
HG_HEAD_DIM = 128
HG_HEADS = HG_WIDTH // HG_HEAD_DIM
MLA_NOPE = 128
MLA_ROPE = 64
MLA_V = 128
MLA_HEADS = 8
MLA_WIDTH = MLA_HEADS * MLA_V
MLA_Q_RANK = 512
MLA_KV_RANK = 256
MLA_QK_DIM = MLA_NOPE + MLA_ROPE
MLA_QK_PAD = 256
MLA_IN = MLA_Q_RANK + MLA_KV_RANK + MLA_ROPE
MLA_IN_PAD = 896
ROPE_THETA = 10000.0
MASK_VALUE = -1e30
RW_WIDTH = 512
RW_HEAD = 64
RW_HEADS = RW_WIDTH // RW_HEAD
RW_PAIRS = RW_HEADS // 2
RW_W_RANK = 64
RW_A_RANK = 64
RW_V_RANK = 32
RW_G_RANK = 128
RW_IN = 3 * RW_WIDTH + RW_W_RANK + RW_A_RANK + RW_G_RANK
RW_GN_EPS = 64e-5
MIX_WIDTH = HG_WIDTH + MLA_WIDTH + RW_WIDTH
D_FF = 5632
RMS_EPS = 1e-6

HG_CHUNK = 32
HG_EXP_CLAMP = 80.0
HG_CHUNKS_PER_STEP = 4
RW_CHUNK = 64
RW_CHUNKS_PER_STEP = 4
ATTN_ROW_CHAINS = 4
LANES = 128

_MIB = 1024 * 1024


def _cparams(semantics, vmem_mib):
    return pltpu.CompilerParams(dimension_semantics=semantics, vmem_limit_bytes=vmem_mib * _MIB)


def _dot(a, b):
    return jnp.dot(a, b, preferred_element_type=F32)


def _dot_nt(a, b):
    return lax.dot_general(a, b, (((1,), (1,)), ((), ())), preferred_element_type=F32)


def _dot_tn(a, b):
    return lax.dot_general(a, b, (((0,), (0,)), ((), ())), preferred_element_type=F32)


def _split(x, terms):
    parts = []
    for _ in range(terms - 1):
        hi = x.astype(BF16)
        parts.append(hi)
        x = x - hi.astype(F32)
    parts.append(x.astype(BF16))
    return parts


def _dot_sel_lhs(sel, x, terms=3):
    return sum(_dot(sel, part) for part in _split(x, terms))


def _dot_sel_rhs(x, sel, terms=3):
    return sum(_dot(part, sel) for part in _split(x, terms))


def _rms(x, width):
    ms = jnp.sum(x * x, axis=-1, keepdims=True) * (1.0 / width)
    return x * lax.rsqrt(ms + RMS_EPS)


def _in_proj_body(x_ref, g_ref, whg_ref, wmla_ref, wrw_ref, ohg_ref, omla_ref, orw_ref):
    h = (_rms(x_ref[...], x_ref.shape[-1]) * g_ref[...]).astype(BF16)
    ohg_ref[...] = _dot(h, whg_ref[...])
    omla_ref[...] = _dot(h, wmla_ref[...])
    orw_ref[...] = _dot(h, wrw_ref[...])


def in_proj(x, gain, w_hg, w_mla, w_rw, *, bm):
    t, d = x.shape
    resident = lambda a: pl.BlockSpec(a.shape, lambda i: (0, 0), pipeline_mode=pl.Buffered(1))
    rows = lambda width: pl.BlockSpec((bm, width), lambda i: (i, 0))
    widths = [w_hg.shape[1], w_mla.shape[1], w_rw.shape[1]]
    return pl.pallas_call(
        _in_proj_body,
        grid=(t // bm,),
        in_specs=[rows(d), pl.BlockSpec((1, d), lambda i: (0, 0)), resident(w_hg), resident(w_mla), resident(w_rw)],
        out_specs=[rows(n) for n in widths],
        out_shape=[jax.ShapeDtypeStruct((t, n), F32) for n in widths],
        compiler_params=_cparams(("parallel",), 56),
        name="in_proj",
    )(x, gain.reshape(1, d), w_hg, w_mla, w_rw)


def _out_proj_body(x_ref, a_ref, b_ref, c_ref, wa_ref, wb_ref, wc_ref, o_ref):
    acc = _dot(a_ref[...], wa_ref[...]) + _dot(b_ref[...], wb_ref[...]) + _dot(c_ref[...], wc_ref[...])
    o_ref[...] = x_ref[...] + acc


def out_proj(x, o_hg, o_mla, o_rw, w_hg, w_mla, w_rw, *, bm):
    t, d = x.shape
    row = lambda width: pl.BlockSpec((bm, width), lambda i: (i, 0))
    resident = lambda a: pl.BlockSpec(a.shape, lambda i: (0, 0), pipeline_mode=pl.Buffered(1))
    return pl.pallas_call(
        _out_proj_body,
        grid=(t // bm,),
        in_specs=[row(d), row(HG_WIDTH), row(MLA_WIDTH), row(RW_WIDTH),
                  resident(w_hg), resident(w_mla), resident(w_rw)],
        out_specs=row(d),
        out_shape=jax.ShapeDtypeStruct((t, d), F32),
        compiler_params=_cparams(("parallel",), 48),
        name="out_proj",
    )(x, o_hg, o_mla, o_rw, w_hg, w_mla, w_rw)


def _ffn_body(x_ref, g_ref, wg_ref, wu_ref, wd_ref, o_ref, h_ref):
    @pl.when(pl.program_id(1) == 0)
    def _():
        x = x_ref[...]
        h_ref[...] = (_rms(x, x.shape[-1]) * g_ref[...]).astype(BF16)
        o_ref[...] = x

    h = h_ref[...]
    gate = _dot(h, wg_ref[...])
    up = _dot(h, wu_ref[...])
    act = (gate * jax.nn.sigmoid(gate) * up).astype(BF16)
    o_ref[...] += _dot(act, wd_ref[...])


def ffn(x, gain, w_gate_up, w_down, *, bm, bf):
    t, d = x.shape
    dff = w_down.shape[0]
    nf = dff // bf
    return pl.pallas_call(
        _ffn_body,
        grid=(t // bm, nf),
        in_specs=[
            pl.BlockSpec((bm, d), lambda i, j: (i, 0)),
            pl.BlockSpec((1, d), lambda i, j: (0, 0)),
            pl.BlockSpec((d, bf), lambda i, j: (0, j)),
            pl.BlockSpec((d, bf), lambda i, j: (0, j + nf)),
            pl.BlockSpec((bf, d), lambda i, j: (j, 0)),
        ],
        out_specs=pl.BlockSpec((bm, d), lambda i, j: (i, 0)),
        out_shape=jax.ShapeDtypeStruct((t, d), F32),
        scratch_shapes=[pltpu.VMEM((bm, d), BF16)],
        compiler_params=_cparams(("parallel", "arbitrary"), 56),
        name="ffn",
    )(x, gain.reshape(1, d), w_gate_up, w_gate_up, w_down)


def _hgrn2_body(p_ref, lb_ref, gain_ref, o_ref, st_ref):
    c = HG_CHUNK
    w = HG_WIDTH

    @pl.when(pl.program_id(1) == 0)
    def _():
        st_ref[...] = jnp.zeros_like(st_ref)

    row = lax.broadcasted_iota(jnp.int32, (c, c), 0)
    col = lax.broadcasted_iota(jnp.int32, (c, c), 1)
    tril = col <= row
    cum = tril.astype(BF16)
    lb = lb_ref[...]
    gain = gain_ref[...]

    heads = [slice(h * HG_HEAD_DIM, (h + 1) * HG_HEAD_DIM) for h in range(HG_HEADS)]

    def chunk_stages(rows, states, u):
        q_raw = p_ref[rows, 0:w]
        z = p_ref[rows, w:2 * w]
        v = p_ref[rows, 2 * w:3 * w]
        g_raw = p_ref[rows, 3 * w:4 * w]
        q = q_raw * jax.nn.sigmoid(q_raw)
        log_f = jnp.log(lb + (1.0 - lb) * jax.nn.sigmoid(z))
        k = (1.0 - lb) * jax.nn.sigmoid(-z)
        b = _dot_sel_lhs(cum, log_f)
        yield
        b_mid = b[c // 2 - 1:c // 2, :]
        b_end = b[c - 1:c, :]
        q_in = (q * jnp.exp(jnp.minimum(b - b_mid, HG_EXP_CLAMP))).astype(BF16)
        k_in = (k * jnp.exp(jnp.minimum(b_mid - b, HG_EXP_CLAMP))).astype(BF16)
        q_st = (q * jnp.exp(b)).astype(BF16)
        k_st = (k * jnp.exp(b_end - b)).astype(BF16)
        d_end = jnp.exp(b_end)
        vb = v.astype(BF16)
        gate = g_raw * jax.nn.sigmoid(g_raw)
        scores = [jnp.where(tril, _dot_nt(q_in[:, sl], k_in[:, sl]), 0.0).astype(BF16) for sl in heads]
        updates = [_dot_tn(vb[:, sl], k_st[:, sl]) for sl in heads]
        yield
        st = states[u]
        states[u + 1] = [st[h] * d_end[:, sl] + updates[h] for h, sl in enumerate(heads)]
        outs = [_dot_nt(q_st[:, sl], st[h].astype(BF16)) + _dot(scores[h], vb[:, sl]) for h, sl in enumerate(heads)]
        yield
        for h, sl in enumerate(heads):
            o_ref[rows, sl] = (_rms(outs[h], HG_HEAD_DIM) * gain * gate[:, sl]).astype(o_ref.dtype)

    def step(si, carry):
        states = [[st_ref[h] for h in range(HG_HEADS)]] + [None] * HG_CHUNKS_PER_STEP
        base = si * HG_CHUNKS_PER_STEP
        _lock_step([chunk_stages(pl.ds(pl.multiple_of((base + u) * c, c), c), states, u)
                    for u in range(HG_CHUNKS_PER_STEP)])
        for h in range(HG_HEADS):
            st_ref[h] = states[HG_CHUNKS_PER_STEP][h]
        return carry

    lax.fori_loop(0, p_ref.shape[0] // (c * HG_CHUNKS_PER_STEP), step, 0)


def hgrn2(p_hg, lower_bound, out_gain, *, batch, bs):
    t = p_hg.shape[0]
    ns = t // batch // bs
    assert t == batch * ns * bs and bs % (HG_CHUNK * HG_CHUNKS_PER_STEP) == 0
    return pl.pallas_call(
        _hgrn2_body,
        grid=(batch, ns),
        in_specs=[
            pl.BlockSpec((bs, 4 * HG_WIDTH), lambda b, s: (b * ns + s, 0)),
            pl.BlockSpec((1, HG_WIDTH), lambda b, s: (0, 0)),
            pl.BlockSpec((1, HG_HEAD_DIM), lambda b, s: (0, 0)),
        ],
        out_specs=pl.BlockSpec((bs, HG_WIDTH), lambda b, s: (b * ns + s, 0)),
        out_shape=jax.ShapeDtypeStruct((t, HG_WIDTH), BF16),
        scratch_shapes=[pltpu.VMEM((HG_HEADS, HG_HEAD_DIM, HG_HEAD_DIM), F32)],
        compiler_params=_cparams(("parallel", "arbitrary"), 32),
        name="hgrn2",
    )(p_hg, lower_bound.reshape(1, HG_WIDTH), out_gain.reshape(1, HG_HEAD_DIM))


def _rope(x, cos, sin):
    half = MLA_ROPE // 2
    lane = lax.broadcasted_iota(jnp.int32, x.shape, 1)
    rot = jnp.where(lane < half, -pltpu.roll(x, LANES - half, 1), pltpu.roll(x, half, 1))
    return x * cos + rot * sin


def _mla_prep_body(p_ref, cos_ref, sin_ref, gqa_ref, wqb_ref, gkva_ref, wkvb_ref, gq_ref, gk_ref,
                   q_ref, k_ref, v_ref):
    p = p_ref[...]
    cos = cos_ref[...]
    sin = sin_ref[...]
    scale = MLA_QK_DIM ** -0.5
    qn = (_rms(p[:, 0:MLA_Q_RANK], MLA_Q_RANK) * gqa_ref[...]).astype(BF16)
    q = _dot(qn, wqb_ref[...])
    kvn = (_rms(p[:, MLA_Q_RANK:MLA_Q_RANK + MLA_KV_RANK], MLA_KV_RANK) * gkva_ref[...]).astype(BF16)
    kv = _dot(kvn, wkvb_ref[...])
    gq_nope, gq_rope = gq_ref[0:1, :], gq_ref[1:2, :]
    gk_nope, gk_rope = gk_ref[0:1, :], gk_ref[1:2, :]
    k_rope = _rope(_rms(p[:, MLA_Q_RANK + MLA_KV_RANK:], MLA_ROPE) * gk_rope, cos, sin).astype(BF16)
    half = MLA_ROPE // 2
    first_half = lax.broadcasted_iota(jnp.int32, cos.shape, 1) < half

    def head_stages(h):
        o = h * MLA_QK_PAD
        q_nope, q_rope, k_nope = q[:, o:o + MLA_NOPE], q[:, o + MLA_NOPE:o + MLA_QK_PAD], kv[:, o:o + MLA_NOPE]
        ss_qn = jnp.sum(q_nope * q_nope, axis=-1, keepdims=True)
        ss_qr = jnp.sum(q_rope * q_rope, axis=-1, keepdims=True)
        ss_kn = jnp.sum(k_nope * k_nope, axis=-1, keepdims=True)
        yield
        q_nope = q_nope * lax.rsqrt(ss_qn * (1.0 / MLA_NOPE) + RMS_EPS) * (gq_nope * scale)
        q_rope = q_rope * lax.rsqrt(ss_qr * (1.0 / MLA_ROPE) + RMS_EPS) * (gq_rope * scale)
        k_nope = k_nope * lax.rsqrt(ss_kn * (1.0 / MLA_NOPE) + RMS_EPS) * gk_nope
        up, down = pltpu.roll(q_rope, LANES - half, 1), pltpu.roll(q_rope, half, 1)
        yield
        q_ref[h, :, 0:MLA_NOPE] = q_nope.astype(BF16)
        q_ref[h, :, MLA_NOPE:MLA_QK_PAD] = (q_rope * cos + jnp.where(first_half, -up, down) * sin).astype(BF16)
        k_ref[h, :, 0:MLA_NOPE] = k_nope.astype(BF16)
        k_ref[h, :, MLA_NOPE:MLA_QK_PAD] = k_rope
        v_ref[h] = kv[:, o + MLA_NOPE:o + MLA_QK_PAD].astype(BF16)

    _lock_step([head_stages(h) for h in range(MLA_HEADS)])


def mla_prep(p_mla, cos, sin, g_qa, w_qb, g_kva, w_kvb, g_q, g_k, *, bm):
    t = p_mla.shape[0]
    full = lambda a: pl.BlockSpec(a.shape, lambda i: (0,) * a.ndim)
    rows = lambda width: pl.BlockSpec((bm, width), lambda i: (i, 0))
    heads = lambda width: pl.BlockSpec((MLA_HEADS, bm, width), lambda i: (0, i, 0))
    g_qa = g_qa.reshape(1, -1)
    g_kva = g_kva.reshape(1, -1)
    return pl.pallas_call(
        _mla_prep_body,
        grid=(t // bm,),
        in_specs=[rows(MLA_IN_PAD), rows(LANES), rows(LANES), full(g_qa), full(w_qb), full(g_kva), full(w_kvb),
                  full(g_q), full(g_k)],
        out_specs=[heads(MLA_QK_PAD), heads(MLA_QK_PAD), heads(MLA_V)],
        out_shape=[jax.ShapeDtypeStruct((MLA_HEADS, t, MLA_QK_PAD), BF16),
                   jax.ShapeDtypeStruct((MLA_HEADS, t, MLA_QK_PAD), BF16),
                   jax.ShapeDtypeStruct((MLA_HEADS, t, MLA_V), BF16)],
        compiler_params=_cparams(("parallel",), 48),
        name="mla_prep",
    )(p_mla, cos, sin, g_qa, w_qb, g_kva, w_kvb, g_q, g_k)


def _pipelined(stages):
    n = len(stages)
    done = [False] * n
    rnd = 0
    while not all(done):
        for i in range(min(rnd, n - 1) + 1):
            if not done[i]:
                try:
                    next(stages[i])
                except StopIteration:
                    done[i] = True
        rnd += 1


def _flash_body(i_tab, j_tab, q_ref, k_ref, v_ref, o_ref, m_ref, l_ref, acc_ref):
    pair = pl.program_id(2)
    i = i_tab[pair]
    j = j_tab[pair]
    blk = q_ref.shape[1]
    rb = blk // ATTN_ROW_CHAINS

    @pl.when(j == 0)
    def _():
        m_ref[...] = jnp.full_like(m_ref, MASK_VALUE)
        l_ref[...] = jnp.zeros_like(l_ref)
        acc_ref[...] = jnp.zeros_like(acc_ref)

    def row_group(r, diagonal):
        rows = slice(r * rb, (r + 1) * rb)
        ncol = (r + 1) * rb if diagonal else blk
        s = _dot_nt(q_ref[0, rows, :], k_ref[0, 0:ncol, :])
        yield
        if diagonal:
            row = lax.broadcasted_iota(jnp.int32, (rb, rb), 0)
            col = lax.broadcasted_iota(jnp.int32, (rb, rb), 1)
            last = jnp.where(col <= row, s[:, r * rb:], MASK_VALUE)
            s = jnp.concatenate([s[:, :r * rb], last], axis=1) if r else last
        m_old = m_ref[rows, :]
        m_new = jnp.maximum(m_old, jnp.max(s, axis=-1, keepdims=True))
        alpha = jnp.exp(m_old - m_new)
        p = jnp.exp(s - jnp.concatenate([m_new] * (ncol // LANES), axis=1))
        l_new = alpha * l_ref[rows, :] + jnp.sum(p, axis=-1, keepdims=True)
        pb = p.astype(BF16)
        yield
        acc = alpha * acc_ref[rows, :] + _dot(pb, v_ref[0, 0:ncol, :])
        if diagonal:
            o_ref[rows, :] = (acc / l_new).astype(o_ref.dtype)
        else:
            m_ref[rows, :] = m_new
            l_ref[rows, :] = l_new
            acc_ref[rows, :] = acc

    @pl.when(j < i)
    def _():
        _pipelined([row_group(r, False) for r in range(ATTN_ROW_CHAINS)])

    @pl.when(j == i)
    def _():
        _pipelined([row_group(r, True) for r in range(ATTN_ROW_CHAINS)])


def flash_attn(q, k, v, *, batch, blk):
    heads, t, _ = q.shape
    nb = t // batch // blk
    assert t == batch * nb * blk and blk % (ATTN_ROW_CHAINS * LANES) == 0
    pairs =[(i, j) for i in range(nb) for j in range(i + 1)]
    i_tab = jnp.array([p[0] for p in pairs], jnp.int32)
    j_tab = jnp.array([p[1] for p in pairs], jnp.int32)
    qmap = lambda b, h, p, it, jt: (h, b * nb + it[p], 0)
    kmap = lambda b, h, p, it, jt: (h, b * nb + jt[p], 0)
    return pl.pallas_call(
        _flash_body,
        grid_spec=pltpu.PrefetchScalarGridSpec(
            num_scalar_prefetch=2,
            grid=(batch, heads, len(pairs)),
            in_specs=[pl.BlockSpec((1, blk, MLA_QK_PAD), qmap),
                      pl.BlockSpec((1, blk, MLA_QK_PAD), kmap),
                      pl.BlockSpec((1, blk, MLA_V), kmap)],
            out_specs=pl.BlockSpec((blk, MLA_V), lambda b, h, p, it, jt: (b * nb + it[p], h)),
            scratch_shapes=[pltpu.VMEM((blk, LANES), F32), pltpu.VMEM((blk, LANES), F32),
                            pltpu.VMEM((blk, MLA_V), F32)],
        ),
        out_shape=jax.ShapeDtypeStruct((t, heads * MLA_V), BF16),
        compiler_params=_cparams(("parallel", "parallel", "arbitrary"), 48),
        name="flash_attn",
    )(i_tab, j_tab, q, k, v)


def _head_sums(x, weight):
    half = 2 * LANES
    gi = lax.broadcasted_iota(jnp.int32, (half, half), 0) // RW_HEAD
    gj = lax.broadcasted_iota(jnp.int32, (half, half), 1) // RW_HEAD
    sel = jnp.where(gi == gj, weight, 0.0).astype(BF16)
    return jnp.concatenate([_dot_sel_rhs(x[:, 0:half], sel, 2), _dot_sel_rhs(x[:, half:], sel, 2)], axis=1)


def _lock_step(stages):
    results = [None] * len(stages)
    live = list(range(len(stages)))
    while live:
        for i in list(live):
            try:
                next(stages[i])
            except StopIteration as stop:
                results[i] = stop.value
                live.remove(i)
    return results


def _unit_lower_inverse(a, block):
    n = a.shape[0]
    row = lax.broadcasted_iota(jnp.int32, (n, n), 0)
    col = lax.broadcasted_iota(jnp.int32, (n, n), 1)
    diff = row ^ col
    t = jnp.where(row == col, 1.0, 0.0) + jnp.where(diff < 2, a, 0.0)
    size = 2
    while size < block:
        off = jnp.where((diff >= size) & (diff < 2 * size), a, 0.0).astype(BF16)
        tb = t.astype(BF16)
        half = _dot(tb, off).astype(BF16)
        yield
        t = t + _dot(half, tb)
        yield
        size *= 2
    return t


def _rw_chunk_body(has_vres, *refs):
    if has_vres:
        (p_ref, prev_ref, mu_ref, w0_ref, w2_ref, a0_ref, a2_ref, g2_ref, kk_ref, ka_ref, rk_ref,
         v0_ref, v1_ref, v2_ref, vf_ref,
         m_ref, nt_ref, dec_ref, qh_ref, oh_ref, gate_ref, bonus_ref,
         r_s, g_s, k_s, v_s, kk_s, b_s) = refs
    else:
        (p_ref, prev_ref, mu_ref, w0_ref, w2_ref, a0_ref, a2_ref, g2_ref, kk_ref, ka_ref, rk_ref,
         m_ref, nt_ref, dec_ref, qh_ref, oh_ref, gate_ref, bonus_ref, vout_ref,
         r_s, g_s, k_s, v_s, kk_s, b_s) = refs
    c = RW_CHUNK
    w = RW_WIDTH
    bt = p_ref.shape[0]

    p = p_ref[...]
    first = pl.program_id(1) == 0
    prev_row = jnp.where(first, 0.0, prev_ref[7:8, :])
    rowi = lax.broadcasted_iota(jnp.int32, p.shape, 0)
    prev = jnp.where(rowi == 0, prev_row, pltpu.roll(p, 1, 0))
    p = p + (prev - p) * mu_ref[...]
    r = p[:, 0:w]
    k = p[:, w:2 * w]
    v = p[:, 2 * w:3 * w]
    wd = p[:, 3 * w:3 * w + RW_W_RANK]
    ad = p[:, 3 * w + RW_W_RANK:3 * w + RW_W_RANK + RW_A_RANK]
    gd = p[:, 3 * w + RW_W_RANK + RW_A_RANK:]
    w_log = -jax.nn.softplus(-(w0_ref[...] + _dot(jnp.tanh(wd).astype(BF16), w2_ref[...]))) - 0.5
    lr = jax.nn.sigmoid(a0_ref[...] + _dot(ad.astype(BF16), a2_ref[...]))
    gate = _dot(jax.nn.sigmoid(gd).astype(BF16), g2_ref[...])
    if has_vres:
        mix = jax.nn.sigmoid(v0_ref[...] + _dot(_dot(v.astype(BF16), v1_ref[...]).astype(BF16), v2_ref[...]))
        v = v + (vf_ref[...] - v) * mix
    else:
        vout_ref[...] = v
    kk = k * kk_ref[...]
    kk = kk / jnp.maximum(jnp.sqrt(_head_sums(kk * kk, 1.0)), 1e-12)
    k = k * (1.0 + (lr - 1.0) * ka_ref[...])
    gate_ref[...] = gate
    bonus_ref[...] = _head_sums(r * k * rk_ref[...], 1.0) * v
    r_s[...] = r
    g_s[...] = -jnp.exp(w_log)
    k_s[...] = k
    v_s[...] = v
    kk_s[...] = kk
    b_s[...] = kk * lr

    n2 = 2 * c
    row = lax.broadcasted_iota(jnp.int32, (n2, n2), 0)
    col = lax.broadcasted_iota(jnp.int32, (n2, n2), 1)
    same_head = (row ^ col) < c
    incl = same_head & (col <= row)
    strict = same_head & (col < row)
    ci_row = lax.broadcasted_iota(jnp.int32, (c, c), 0)
    ci_col = lax.broadcasted_iota(jnp.int32, (c, c), 1)
    cum = (ci_col <= ci_row).astype(BF16)
    lane = lax.broadcasted_iota(jnp.int32, (1, LANES), 1)
    head0 = lane < RW_HEAD

    def stack(z):
        return jnp.concatenate([jnp.where(head0, z, 0.0), jnp.where(head0, 0.0, z)], axis=0)

    def pair_matrices(g, r, k, v, kkc, bc):
        gc = _dot_sel_lhs(cum, g)
        yield
        g_end = gc[c - 1:c, :]
        inv = jnp.exp(-gc)
        to_end = jnp.exp(g_end - gc)
        a2 = stack(-kkc * jnp.exp(gc - g)).astype(BF16)
        r2 = stack(r * jnp.exp(gc))
        v2 = stack(v).astype(BF16)
        b_end = stack(bc * to_end).astype(BF16)
        k_end = stack(k * to_end).astype(BF16)
        b_inv = (bc * inv).astype(BF16)
        k_inv = (k * inv).astype(BF16)
        ar = jnp.concatenate([a2, r2.astype(BF16)], axis=0)
        abk = _dot_nt(ar, jnp.concatenate([b_inv, b_inv, k_inv, k_inv], axis=0))
        yield
        a_ab = jnp.where(strict, abk[0:n2, 0:n2], 0.0)
        a_rb = jnp.where(incl, abk[n2:2 * n2, 0:n2], 0.0).astype(BF16)
        a_ak = jnp.where(strict, abk[0:n2, n2:2 * n2], 0.0).astype(BF16)
        a_rk = jnp.where(incl, abk[n2:2 * n2, n2:2 * n2], 0.0).astype(BF16)
        x = _dot(a_ak, v2).astype(BF16)
        nt = _dot_tn(v2, k_end)
        oh2 = _dot(a_rk, v2)
        t_inv = (yield from _unit_lower_inverse(a_ab, c)).astype(BF16)
        p12 = _dot(t_inv, jnp.concatenate([a2, x], axis=1)).astype(BF16)
        yield
        qo = _dot(a_rb, p12)
        qh2 = r2 + qo[:, 0:LANES]
        oh2 = oh2 + qo[:, LANES:2 * LANES]
        m = _dot_tn(b_end, p12[:, 0:LANES])
        nt = nt + _dot_tn(p12[:, LANES:2 * LANES], b_end)
        return m, nt, jnp.exp(g_end), qh2[0:c] + qh2[c:n2], oh2[0:c] + oh2[c:n2]

    def chunk(ci, carry):
        units = [(ci * RW_CHUNKS_PER_STEP + u, pr, slice(pr * LANES, (pr + 1) * LANES))
                 for u in range(RW_CHUNKS_PER_STEP) for pr in range(RW_PAIRS)]
        rows = lambda cj: pl.ds(pl.multiple_of(cj * c, c), c)
        loaded = [[s[rows(cj), sl] for s in (g_s, r_s, k_s, v_s, kk_s, b_s)] for cj, _, sl in units]
        results = _lock_step([pair_matrices(*operands) for operands in loaded])
        for (cj, pr, sl), (m, nt, dec, qh, oh) in zip(units, results):
            m_ref[cj, pr] = m.astype(m_ref.dtype)
            nt_ref[cj, pr] = nt
            dec_ref[cj, pr] = dec
            qh_ref[rows(cj), sl] = qh.astype(qh_ref.dtype)
            oh_ref[rows(cj), sl] = oh
        return carry

    lax.fori_loop(0, bt // c // RW_CHUNKS_PER_STEP, chunk, 0)


def rw_chunk(p_rw, mu, w0, w2, a0, a2, g2, k_k, k_a, r_k, vres, *, batch, bt):
    t = p_rw.shape[0]
    nb = t // batch // bt
    nc = bt // RW_CHUNK
    assert t == batch * nb * bt and bt % (RW_CHUNK * RW_CHUNKS_PER_STEP) == 0
    has_vres = vres is not None
    row1 = lambda a: a.reshape(1, -1)
    full = lambda a: pl.BlockSpec(a.shape, lambda b, s: (0,) * a.ndim)
    rows = lambda width: pl.BlockSpec((bt, width), lambda b, s: (b * nb + s, 0))
    prev_spec = pl.BlockSpec((8, RW_IN), lambda b, s: (jnp.maximum((b * nb + s) * (bt // 8) - 1, 0), 0))
    mats = pl.BlockSpec((nc, RW_PAIRS, LANES, LANES), lambda b, s: (b * nb + s, 0, 0, 0))
    params = [row1(mu), row1(w0), w2, row1(a0), a2, g2, row1(k_k), row1(k_a), row1(r_k)]
    args = [p_rw, p_rw] + params
    in_specs = [rows(RW_IN), prev_spec] + [full(a) for a in params]
    if has_vres:
        v0, v1, v2, v_first = vres
        extra = [row1(v0), v1, v2]
        args += extra + [v_first]
        in_specs += [full(a) for a in extra] + [rows(RW_WIDTH)]
    decs = pl.BlockSpec((nc, RW_PAIRS, 1, LANES), lambda b, s: (b * nb + s, 0, 0, 0))
    mat_shape = lambda dt: jax.ShapeDtypeStruct((t // RW_CHUNK, RW_PAIRS, LANES, LANES), dt)
    dec_shape = jax.ShapeDtypeStruct((t // RW_CHUNK, RW_PAIRS, 1, LANES), F32)
    tok_shape = lambda dt: jax.ShapeDtypeStruct((t, RW_WIDTH), dt)
    out_specs = [mats, mats, decs, rows(RW_WIDTH), rows(RW_WIDTH), rows(RW_WIDTH), rows(RW_WIDTH)]
    out_shape = [mat_shape(BF16), mat_shape(F32), dec_shape, tok_shape(BF16), tok_shape(F32), tok_shape(F32),
                 tok_shape(F32)]
    if not has_vres:
        out_specs.append(rows(RW_WIDTH))
        out_shape.append(tok_shape(F32))
    return pl.pallas_call(
        functools.partial(_rw_chunk_body, has_vres),
        grid=(batch, nb),
        in_specs=in_specs,
        out_specs=out_specs,
        out_shape=out_shape,
        scratch_shapes=[pltpu.VMEM((bt, RW_WIDTH), F32) for _ in range(6)],
        compiler_params=_cparams(("parallel", "arbitrary"), 48),
        name="rw_chunk",
    )(*args)


def _rw_scan_body(m_ref, nt_ref, dec_ref, qh_ref, oh_ref, gate_ref, bonus_ref, lnw_ref, lnb_ref, o_ref, y_ref, h_ref):
    c = RW_CHUNK

    @pl.when(pl.program_id(1) == 0)
    def _():
        h_ref[...] = jnp.zeros_like(h_ref)

    def chunk(ci, carry):
        rows = pl.ds(pl.multiple_of(ci * c, c), c)
        for pr in range(RW_PAIRS):
            sl = slice(pr * LANES, (pr + 1) * LANES)
            ht = h_ref[pr]
            hb = ht.astype(BF16)
            y_ref[rows, sl] = _dot_nt(qh_ref[rows, sl], hb) + oh_ref[rows, sl]
            h_ref[pr] = ht * dec_ref[ci, pr] + _dot_nt(hb, m_ref[ci, pr]) + nt_ref[ci, pr]
        return carry

    lax.fori_loop(0, qh_ref.shape[0] // c, chunk, 0)

    y = y_ref[...]
    mean = _head_sums(y, 1.0 / RW_HEAD)
    d = y - mean
    var = _head_sums(d * d, 1.0 / RW_HEAD)
    y = d * lax.rsqrt(var + RW_GN_EPS) * lnw_ref[...] + lnb_ref[...]
    o_ref[...] = ((y + bonus_ref[...]) * gate_ref[...]).astype(o_ref.dtype)


def rw_scan(m, nt, dec, qh, oh, gate, bonus, ln_w, ln_b, *, batch, bs):
    t = qh.shape[0]
    nb = t // batch // bs
    nc = bs // RW_CHUNK
    rows = pl.BlockSpec((bs, RW_WIDTH), lambda b, s: (b * nb + s, 0))
    mats = pl.BlockSpec((nc, RW_PAIRS, LANES, LANES), lambda b, s: (b * nb + s, 0, 0, 0))
    decs = pl.BlockSpec((nc, RW_PAIRS, 1, LANES), lambda b, s: (b * nb + s, 0, 0, 0))
    vec = pl.BlockSpec((1, RW_WIDTH), lambda b, s: (0, 0))
    return pl.pallas_call(
        _rw_scan_body,
        grid=(batch, nb),
        in_specs=[mats, mats, decs, rows, rows, rows, rows, vec, vec],
        out_specs=rows,
        out_shape=jax.ShapeDtypeStruct((t, RW_WIDTH), BF16),
        scratch_shapes=[pltpu.VMEM((bs, RW_WIDTH), F32), pltpu.VMEM((RW_PAIRS, LANES, LANES), F32)],
        compiler_params=_cparams(("parallel", "arbitrary"), 48),
        name="rw_scan",
    )(m, nt, dec, qh, oh, gate, bonus, ln_w.reshape(1, -1), ln_b.reshape(1, -1))


def _rope_tables(positions):
    half = MLA_ROPE // 2
    inv_freq = ROPE_THETA ** (-jnp.arange(half, dtype=F32) / half)
    ang = positions.astype(F32).reshape(-1, 1) * inv_freq
    zeros = jnp.zeros((ang.shape[0], LANES - MLA_ROPE), F32)
    cos = jnp.concatenate([jnp.cos(ang), jnp.cos(ang), zeros], axis=1)
    sin = jnp.concatenate([jnp.sin(ang), jnp.sin(ang), zeros], axis=1)
    return cos, sin


def _pad_cols(a, width):
    return jnp.pad(a, [(0, 0)] * (a.ndim - 1) + [(0, width - a.shape[-1])])


def _forward(x, positions, attn_norm, w_in, hg_lower_bounds, hg_out_norm,
             mla_q_a_norm, mla_w_qb, mla_kv_a_norm, mla_w_kvb, mla_q_norm, mla_k_norm,
             rw_mu, rw_w0, rw_w2, rw_a0, rw_a2, rw_g2, rw_v0, rw_v1, rw_v2,
             rw_k_k, rw_k_a, rw_r_k, rw_ln_w, rw_ln_b,
             w_o, ffn_norm, w_gate_up, w_down, *, cfg):
    batch, seq, d = x.shape
    depth = w_in.shape[0]
    t = batch * seq
    x = x.reshape(t, d)

    lb_sm = jax.nn.softmax(hg_lower_bounds.astype(F32), axis=0)
    lower_bounds = jnp.cumsum(lb_sm, axis=0) - lb_sm[0]
    cos, sin = _rope_tables(positions)

    hg_end = 4 * HG_WIDTH
    mla_end = hg_end + MLA_IN
    w_in_hg = w_in[:, :, :hg_end].astype(BF16)
    w_in_mla = _pad_cols(w_in[:, :, hg_end:mla_end], MLA_IN_PAD).astype(BF16)
    w_in_rw = w_in[:, :, mla_end:].astype(BF16)
    w_qb = _pad_cols(mla_w_qb.reshape(depth, MLA_Q_RANK, MLA_HEADS, MLA_QK_DIM), MLA_QK_PAD)
    w_qb = w_qb.reshape(depth, MLA_Q_RANK, MLA_HEADS * MLA_QK_PAD).astype(BF16)
    w_kvb = mla_w_kvb.astype(BF16)
    split_gain = lambda g: jnp.stack([g[:, :MLA_NOPE], _pad_cols(g[:, MLA_NOPE:], LANES)], axis=1)
    g_q = split_gain(mla_q_norm)
    g_k = split_gain(mla_k_norm)
    w_o_hg = w_o[:, :HG_WIDTH].astype(BF16)
    w_o_mla = w_o[:, HG_WIDTH:HG_WIDTH + MLA_WIDTH].astype(BF16)
    w_o_rw = w_o[:, HG_WIDTH + MLA_WIDTH:].astype(BF16)
    w_gu = w_gate_up.astype(BF16)
    w_dn = w_down.astype(BF16)
    rw_w2b, rw_a2b, rw_g2b = rw_w2.astype(BF16), rw_a2.astype(BF16), rw_g2.astype(BF16)
    rw_v1b, rw_v2b = rw_v1.astype(BF16), rw_v2.astype(BF16)
    r_k = rw_r_k.reshape(depth, RW_WIDTH)

    v_first = None
    for l in range(depth):
        p_hg, p_mla, p_rw = in_proj(x, attn_norm[l], w_in_hg[l], w_in_mla[l], w_in_rw[l], bm=cfg["proj_bm"])

        o_hg = hgrn2(p_hg, lower_bounds[l], hg_out_norm[l], batch=batch, bs=cfg["hg_bs"])

        q, k, v = mla_prep(p_mla, cos, sin, mla_q_a_norm[l], w_qb[l], mla_kv_a_norm[l], w_kvb[l], g_q[l], g_k[l],
                           bm=cfg["mla_bm"])
        o_mla = flash_attn(q, k, v, batch=batch, blk=cfg["attn_blk"])

        vres = None if l == 0 else (rw_v0[l - 1], rw_v1b[l - 1], rw_v2b[l - 1], v_first)
        outs = rw_chunk(p_rw, rw_mu[l], rw_w0[l], rw_w2b[l], rw_a0[l], rw_a2b[l], rw_g2b[l],
                        rw_k_k[l], rw_k_a[l], r_k[l], vres, batch=batch, bt=cfg["rw_bt"])
        if l == 0:
            v_first = outs[7]
        o_rw = rw_scan(*outs[:7], rw_ln_w[l], rw_ln_b[l], batch=batch, bs=cfg["rw_bs"])

        x = out_proj(x, o_hg, o_mla, o_rw, w_o_hg[l], w_o_mla[l], w_o_rw[l], bm=cfg["out_bm"])
        x = ffn(x, ffn_norm[l], w_gu[l], w_dn[l], bm=cfg["ffn_bm"], bf=cfg["ffn_bf"])
    return x.reshape(batch, seq, d)


_CFG = dict(proj_bm=256, hg_bs=512, mla_bm=256, attn_blk=1024, rw_bt=256, rw_bs=512,
            out_bm=512, ffn_bm=1024, ffn_bf=512)


def kernel(x, positions, attn_norm, w_in, hg_lower_bounds, hg_out_norm, mla_q_a_norm, mla_w_qb, mla_kv_a_norm,
           mla_w_kvb, mla_q_norm, mla_k_norm, rw_mu, rw_w0, rw_w2, rw_a0, rw_a2, rw_g2, rw_v0, rw_v1, rw_v2,
           rw_k_k, rw_k_a, rw_r_k, rw_ln_w, rw_ln_b, w_o, ffn_norm, w_gate_up, w_down):
    return _forward(x, positions, attn_norm, w_in, hg_lower_bounds, hg_out_norm, mla_q_a_norm, mla_w_qb,
                    mla_kv_a_norm, mla_w_kvb, mla_q_norm, mla_k_norm, rw_mu, rw_w0, rw_w2, rw_a0, rw_a2, rw_g2,
                    rw_v0, rw_v1, rw_v2, rw_k_k, rw_k_a, rw_r_k, rw_ln_w, rw_ln_b, w_o, ffn_norm, w_gate_up,
                    w_down, cfg=_CFG)
```

```python
import functools

import jax
import jax.numpy as jnp
from jax import lax
from jax.experimental import pallas as pl
from jax.experimental.pallas import tpu as pltpu

F32 = jnp.float32
BF16 = jnp.bfloat16

D_MODEL = 2048
DEPTH = 4
HG_WIDTH = 512
HG_HEAD_DIM = 128
HG_HEADS = HG_WIDTH // HG_HEAD_DIM
MLA_NOPE = 128
MLA_ROPE = 64
MLA_V = 128
MLA_HEADS = 8
MLA_WIDTH = MLA_HEADS * MLA_V
MLA_Q_RANK = 512
MLA_KV_RANK = 256
MLA_QK_DIM = MLA_NOPE + MLA_ROPE
MLA_QK_PAD = 256
MLA_IN = MLA_Q_RANK + MLA_KV_RANK + MLA_ROPE
MLA_IN_PAD = 896
ROPE_THETA = 10000.0
MASK_VALUE = -1e30
RW_WIDTH = 512
RW_HEAD = 64
RW_HEADS = RW_WIDTH // RW_HEAD
RW_PAIRS = RW_HEADS // 2
RW_W_RANK = 64
RW_A_RANK = 64
RW_V_RANK = 32
RW_G_RANK = 128
RW_IN = 3 * RW_WIDTH + RW_W_RANK + RW_A_RANK + RW_G_RANK
RW_GN_EPS = 64e-5
MIX_WIDTH = HG_WIDTH + MLA_WIDTH + RW_WIDTH
D_FF = 5632
RMS_EPS = 1e-6

HG_CHUNK = 32
HG_EXP_CLAMP = 80.0
HG_CHUNKS_PER_STEP = 4
RW_CHUNK = 64
RW_CHUNKS_PER_STEP = 4
ATTN_ROW_CHAINS = 4
PROJ_COLS = 512
LANES = 128

_MIB = 1024 * 1024


def _cparams(semantics, vmem_mib):
    return pltpu.CompilerParams(dimension_semantics=semantics, vmem_limit_bytes=vmem_mib * _MIB)


def _dot(a, b):
    return jnp.dot(a, b, preferred_element_type=F32)


def _dot_nt(a, b):
    return lax.dot_general(a, b, (((1,), (1,)), ((), ())), preferred_element_type=F32)


def _dot_tn(a, b):
    return lax.dot_general(a, b, (((0,), (0,)), ((), ())), preferred_element_type=F32)


def _split(x, terms):
    parts = []
    for _ in range(terms - 1):
        hi = x.astype(BF16)
        parts.append(hi)
        x = x - hi.astype(F32)
    parts.append(x.astype(BF16))
    return parts


def _dot_sel_lhs(sel, x, terms=3):
    return sum(_dot(sel, part) for part in _split(x, terms))


def _dot_sel_rhs(x, sel, terms=3):
    return sum(_dot(part, sel) for part in _split(x, terms))


def _rms(x, width):
    ms = jnp.sum(x * x, axis=-1, keepdims=True) * (1.0 / width)
    return x * lax.rsqrt(ms + RMS_EPS)


def _out_proj_body(x_ref, a_ref, b_ref, c_ref, wa_ref, wb_ref, wc_ref, o_ref):
    acc = _dot(a_ref[...], wa_ref[...]) + _dot(b_ref[...], wb_ref[...]) + _dot(c_ref[...], wc_ref[...])
    o_ref[...] = x_ref[...] + acc


def out_proj(x, o_hg, o_mla, o_rw, w_hg, w_mla, w_rw, *, bm):
    t, d = x.shape
    row = lambda width: pl.BlockSpec((bm, width), lambda i: (i, 0))
    resident = lambda a: pl.BlockSpec(a.shape, lambda i: (0, 0), pipeline_mode=pl.Buffered(1))
    return pl.pallas_call(
        _out_proj_body,
        grid=(t // bm,),
        in_specs=[row(d), row(HG_WIDTH), row(MLA_WIDTH), row(RW_WIDTH),
                  resident(w_hg), resident(w_mla), resident(w_rw)],
        out_specs=row(d),
        out_shape=jax.ShapeDtypeStruct((t, d), F32),
        compiler_params=_cparams(("parallel",), 48),
        name="out_proj",
    )(x, o_hg, o_mla, o_rw, w_hg, w_mla, w_rw)


def _ffn_body(x_ref, g_ref, wg_ref, wu_ref, wd_ref, o_ref, h_ref):
    @pl.when(pl.program_id(1) == 0)
    def _():
        x = x_ref[...]
        h_ref[...] = (_rms(x, x.shape[-1]) * g_ref[...]).astype(BF16)
        o_ref[...] = x

    h = h_ref[...]
    gate = _dot(h, wg_ref[...])
    up = _dot(h, wu_ref[...])
    act = (gate * jax.nn.sigmoid(gate) * up).astype(BF16)
    o_ref[...] += _dot(act, wd_ref[...])


def ffn(x, gain, w_gate_up, w_down, *, bm, bf):
    t, d = x.shape
    dff = w_down.shape[0]
    nf = dff // bf
    return pl.pallas_call(
        _ffn_body,
        grid=(t // bm, nf),
        in_specs=[
            pl.BlockSpec((bm, d), lambda i, j: (i, 0)),
            pl.BlockSpec((1, d), lambda i, j: (0, 0)),
            pl.BlockSpec((d, bf), lambda i, j: (0, j)),
            pl.BlockSpec((d, bf), lambda i, j: (0, j + nf)),
            pl.BlockSpec((bf, d), lambda i, j: (j, 0)),
        ],
        out_specs=pl.BlockSpec((bm, d), lambda i, j: (i, 0)),
        out_shape=jax.ShapeDtypeStruct((t, d), F32),
        scratch_shapes=[pltpu.VMEM((bm, d), BF16)],
        compiler_params=_cparams(("parallel", "arbitrary"), 56),
        name="ffn",
    )(x, gain.reshape(1, d), w_gate_up, w_gate_up, w_down)


def _hgrn2_stages(p_ref, lb_ref, gain_ref, o_ref, st_ref, first_chunk):
    c = HG_CHUNK
    w = HG_WIDTH
    row = lax.broadcasted_iota(jnp.int32, (c, c), 0)
    col = lax.broadcasted_iota(jnp.int32, (c, c), 1)
    tril = col <= row
    cum = tril.astype(BF16)
    lb = lb_ref[...]
    gain = gain_ref[...]
    heads = [slice(h * HG_HEAD_DIM, (h + 1) * HG_HEAD_DIM) for h in range(HG_HEADS)]
    states = [[st_ref[h] for h in range(HG_HEADS)]] + [None] * HG_CHUNKS_PER_STEP

    def chunk_stages(rows, u):
        q_raw = p_ref[rows, 0:w]
        z = p_ref[rows, w:2 * w]
        v = p_ref[rows, 2 * w:3 * w]
        g_raw = p_ref[rows, 3 * w:4 * w]
        q = q_raw * jax.nn.sigmoid(q_raw)
        log_f = jnp.log(lb + (1.0 - lb) * jax.nn.sigmoid(z))
        k = (1.0 - lb) * jax.nn.sigmoid(-z)
        b = _dot_sel_lhs(cum, log_f)
        yield
        b_mid = b[c // 2 - 1:c // 2, :]
        b_end = b[c - 1:c, :]
        q_in = (q * jnp.exp(jnp.minimum(b - b_mid, HG_EXP_CLAMP))).astype(BF16)
        k_in = (k * jnp.exp(jnp.minimum(b_mid - b, HG_EXP_CLAMP))).astype(BF16)
        q_st = (q * jnp.exp(b)).astype(BF16)
        k_st = (k * jnp.exp(b_end - b)).astype(BF16)
        d_end = jnp.exp(b_end)
        vb = v.astype(BF16)
        gate = g_raw * jax.nn.sigmoid(g_raw)
        scores = [jnp.where(tril, _dot_nt(q_in[:, sl], k_in[:, sl]), 0.0).astype(BF16) for sl in heads]
        updates = [_dot_tn(vb[:, sl], k_st[:, sl]) for sl in heads]
        yield
        st = states[u]
        states[u + 1] = [st[h] * d_end[:, sl] + updates[h] for h, sl in enumerate(heads)]
        outs = [_dot_nt(q_st[:, sl], st[h].astype(BF16)) + _dot(scores[h], vb[:, sl]) for h, sl in enumerate(heads)]
        yield
        for h, sl in enumerate(heads):
            o_ref[rows, sl] = (_rms(outs[h], HG_HEAD_DIM) * gain * gate[:, sl]).astype(o_ref.dtype)

    yield from _rounds([chunk_stages(pl.ds(pl.multiple_of((first_chunk + u) * c, c), c), u)
                        for u in range(HG_CHUNKS_PER_STEP)])
    for h in range(HG_HEADS):
        st_ref[h] = states[HG_CHUNKS_PER_STEP][h]


def _rope(x, cos, sin):
    half = MLA_ROPE // 2
    lane = lax.broadcasted_iota(jnp.int32, x.shape, 1)
    rot = jnp.where(lane < half, -pltpu.roll(x, LANES - half, 1), pltpu.roll(x, half, 1))
    return x * cos + rot * sin


def _mla_stages(p_ref, cos_ref, sin_ref, gqa_ref, wqb_ref, gkva_ref, wkvb_ref, gq_ref, gk_ref, q_ref, k_ref, v_ref):
    p = p_ref[...]
    cos = cos_ref[...]
    sin = sin_ref[...]
    scale = MLA_QK_DIM ** -0.5
    qn = (_rms(p[:, 0:MLA_Q_RANK], MLA_Q_RANK) * gqa_ref[...]).astype(BF16)
    q = _dot(qn, wqb_ref[...])
    kvn = (_rms(p[:, MLA_Q_RANK:MLA_Q_RANK + MLA_KV_RANK], MLA_KV_RANK) * gkva_ref[...]).astype(BF16)
    kv = _dot(kvn, wkvb_ref[...])
    gq_nope, gq_rope = gq_ref[0:1, :], gq_ref[1:2, :]
    gk_nope, gk_rope = gk_ref[0:1, :], gk_ref[1:2, :]
    k_rope = _rope(_rms(p[:, MLA_Q_RANK + MLA_KV_RANK:], MLA_ROPE) * gk_rope, cos, sin).astype(BF16)
    half = MLA_ROPE // 2
    first_half = lax.broadcasted_iota(jnp.int32, cos.shape, 1) < half
    yield

    def head_stages(h):
        o = h * MLA_QK_PAD
        q_nope, q_rope, k_nope = q[:, o:o + MLA_NOPE], q[:, o + MLA_NOPE:o + MLA_QK_PAD], kv[:, o:o + MLA_NOPE]
        ss_qn = jnp.sum(q_nope * q_nope, axis=-1, keepdims=True)
        ss_qr = jnp.sum(q_rope * q_rope, axis=-1, keepdims=True)
        ss_kn = jnp.sum(k_nope * k_nope, axis=-1, keepdims=True)
        yield
        q_nope = q_nope * lax.rsqrt(ss_qn * (1.0 / MLA_NOPE) + RMS_EPS) * (gq_nope * scale)
        q_rope = q_rope * lax.rsqrt(ss_qr * (1.0 / MLA_ROPE) + RMS_EPS) * (gq_rope * scale)
        k_nope = k_nope * lax.rsqrt(ss_kn * (1.0 / MLA_NOPE) + RMS_EPS) * gk_nope
        up, down = pltpu.roll(q_rope, LANES - half, 1), pltpu.roll(q_rope, half, 1)
        yield
        q_ref[h, :, 0:MLA_NOPE] = q_nope.astype(BF16)
        q_ref[h, :, MLA_NOPE:MLA_QK_PAD] = (q_rope * cos + jnp.where(first_half, -up, down) * sin).astype(BF16)
        k_ref[h, :, 0:MLA_NOPE] = k_nope.astype(BF16)
        k_ref[h, :, MLA_NOPE:MLA_QK_PAD] = k_rope
        v_ref[h] = kv[:, o + MLA_NOPE:o + MLA_QK_PAD].astype(BF16)

    yield from _rounds([head_stages(h) for h in range(MLA_HEADS)])


def _mixer_in_body(blocks_per_seq, x_ref, g_ref, whg_ref, wmla_ref, wrw_ref,
                   cos_ref, sin_ref, gqa_ref, wqb_ref, gkva_ref, wkvb_ref, gq_ref, gk_ref, lb_ref, hgain_ref,
                   prw_ref, q_ref, k_ref, v_ref, ohg_ref,
                   h_s, phg_s, pmla_s, st_ref):
    @pl.when(pl.program_id(0) % blocks_per_seq == 0)
    def _():
        st_ref[...] = jnp.zeros_like(st_ref)

    h_s[...] = (_rms(x_ref[...], x_ref.shape[-1]) * g_ref[...]).astype(BF16)

    def projections():
        pmla_s[...] = _dot(h_s[...], wmla_ref[...])
        yield
        for c0 in range(0, whg_ref.shape[1], PROJ_COLS):
            phg_s[:, c0:c0 + PROJ_COLS] = _dot(h_s[...], whg_ref[:, c0:c0 + PROJ_COLS])
            yield
        for c0 in range(0, wrw_ref.shape[1], PROJ_COLS):
            c1 = min(c0 + PROJ_COLS, wrw_ref.shape[1])
            prw_ref[:, c0:c1] = _dot(h_s[...], wrw_ref[:, c0:c1])
            yield

    def hgrn2_blocks():
        for first in range(0, x_ref.shape[0] // HG_CHUNK, HG_CHUNKS_PER_STEP):
            yield from _hgrn2_stages(phg_s, lb_ref, hgain_ref, ohg_ref, st_ref, first)

    proj = projections()
    next(proj)
    _alongside(proj, whg_ref.shape[1] // PROJ_COLS,
               _mla_stages(pmla_s, cos_ref, sin_ref, gqa_ref, wqb_ref, gkva_ref, wkvb_ref, gq_ref, gk_ref,
                           q_ref, k_ref, v_ref))
    _alongside(proj, pl.cdiv(wrw_ref.shape[1], PROJ_COLS), hgrn2_blocks())


def mixer_in(x, gain, w_hg, w_mla, w_rw, cos, sin, g_qa, w_qb, g_kva, w_kvb, g_q, g_k, lower_bound, out_gain,
             *, batch, bm):
    t, d = x.shape
    assert t % (batch * bm) == 0 and bm % (HG_CHUNK * HG_CHUNKS_PER_STEP) == 0
    resident = lambda a: pl.BlockSpec(a.shape, lambda i: (0,) * a.ndim, pipeline_mode=pl.Buffered(1))
    rows = lambda width: pl.BlockSpec((bm, width), lambda i: (i, 0))
    heads = lambda width: pl.BlockSpec((MLA_HEADS, bm, width), lambda i: (0, i, 0))
    params = [gain.reshape(1, d), w_hg, w_mla, w_rw]
    mla_params = [g_qa.reshape(1, -1), w_qb, g_kva.reshape(1, -1), w_kvb, g_q, g_k]
    hg_params = [lower_bound.reshape(1, HG_WIDTH), out_gain.reshape(1, HG_HEAD_DIM)]
    return pl.pallas_call(
        functools.partial(_mixer_in_body, t // batch // bm),
        grid=(t // bm,),
        in_specs=([rows(d)] + [resident(a) for a in params] + [rows(LANES), rows(LANES)]
                  + [resident(a) for a in mla_params + hg_params]),
        out_specs=[rows(w_rw.shape[1]), heads(MLA_QK_PAD), heads(MLA_QK_PAD), heads(MLA_V), rows(HG_WIDTH)],
        out_shape=[jax.ShapeDtypeStruct((t, w_rw.shape[1]), F32),
                   jax.ShapeDtypeStruct((MLA_HEADS, t, MLA_QK_PAD), BF16),
                   jax.ShapeDtypeStruct((MLA_HEADS, t, MLA_QK_PAD), BF16),
                   jax.ShapeDtypeStruct((MLA_HEADS, t, MLA_V), BF16),
                   jax.ShapeDtypeStruct((t, HG_WIDTH), BF16)],
        scratch_shapes=[pltpu.VMEM((bm, d), BF16), pltpu.VMEM((bm, w_hg.shape[1]), F32),
                        pltpu.VMEM((bm, w_mla.shape[1]), F32),
                        pltpu.VMEM((HG_HEADS, HG_HEAD_DIM, HG_HEAD_DIM), F32)],
        compiler_params=_cparams(("arbitrary",), 56),
        name="mixer_in",
    )(x, *params, cos, sin, *mla_params, *hg_params)


def _pipelined(stages):
    n = len(stages)
    done = [False] * n
    rnd = 0
    while not all(done):
        for i in range(min(rnd, n - 1) + 1):
            if not done[i]:
                try:
                    next(stages[i])
                except StopIteration:
                    done[i] = True
        rnd += 1


def _flash_body(i_tab, j_tab, q_ref, k_ref, v_ref, o_ref, m_ref, l_ref, acc_ref):
    pair = pl.program_id(2)
    i = i_tab[pair]
    j = j_tab[pair]
    blk = q_ref.shape[1]
    rb = blk // ATTN_ROW_CHAINS

    @pl.when(j == 0)
    def _():
        m_ref[...] = jnp.full_like(m_ref, MASK_VALUE)
        l_ref[...] = jnp.zeros_like(l_ref)
        acc_ref[...] = jnp.zeros_like(acc_ref)

    def row_group(r, diagonal):
        rows = slice(r * rb, (r + 1) * rb)
        ncol = (r + 1) * rb if diagonal else blk
        s = _dot_nt(q_ref[0, rows, :], k_ref[0, 0:ncol, :])
        yield
        if diagonal:
            row = lax.broadcasted_iota(jnp.int32, (rb, rb), 0)
            col = lax.broadcasted_iota(jnp.int32, (rb, rb), 1)
            last = jnp.where(col <= row, s[:, r * rb:], MASK_VALUE)
            s = jnp.concatenate([s[:, :r * rb], last], axis=1) if r else last
        m_old = m_ref[rows, :]
        m_new = jnp.maximum(m_old, jnp.max(s, axis=-1, keepdims=True))
        alpha = jnp.exp(m_old - m_new)
        p = jnp.exp(s - jnp.concatenate([m_new] * (ncol // LANES), axis=1))
        l_new = alpha * l_ref[rows, :] + jnp.sum(p, axis=-1, keepdims=True)
        pb = p.astype(BF16)
        yield
        acc = alpha * acc_ref[rows, :] + _dot(pb, v_ref[0, 0:ncol, :])
        if diagonal:
            o_ref[rows, :] = (acc / l_new).astype(o_ref.dtype)
        else:
            m_ref[rows, :] = m_new
            l_ref[rows, :] = l_new
            acc_ref[rows, :] = acc

    @pl.when(j < i)
    def _():
        _pipelined([row_group(r, False) for r in range(ATTN_ROW_CHAINS)])

    @pl.when(j == i)
    def _():
        _pipelined([row_group(r, True) for r in range(ATTN_ROW_CHAINS)])


def flash_attn(q, k, v, *, batch, blk):
    heads, t, _ = q.shape
    nb = t // batch // blk
    assert t == batch * nb * blk and blk % (ATTN_ROW_CHAINS * LANES) == 0
    pairs =[(i, j) for i in range(nb) for j in range(i + 1)]
    i_tab = jnp.array([p[0] for p in pairs], jnp.int32)
    j_tab = jnp.array([p[1] for p in pairs], jnp.int32)
    qmap = lambda b, h, p, it, jt: (h, b * nb + it[p], 0)
    kmap = lambda b, h, p, it, jt: (h, b * nb + jt[p], 0)
    return pl.pallas_call(
        _flash_body,
        grid_spec=pltpu.PrefetchScalarGridSpec(
            num_scalar_prefetch=2,
            grid=(batch, heads, len(pairs)),
            in_specs=[pl.BlockSpec((1, blk, MLA_QK_PAD), qmap),
                      pl.BlockSpec((1, blk, MLA_QK_PAD), kmap),
                      pl.BlockSpec((1, blk, MLA_V), kmap)],
            out_specs=pl.BlockSpec((blk, MLA_V), lambda b, h, p, it, jt: (b * nb + it[p], h)),
            scratch_shapes=[pltpu.VMEM((blk, LANES), F32), pltpu.VMEM((blk, LANES), F32),
                            pltpu.VMEM((blk, MLA_V), F32)],
        ),
        out_shape=jax.ShapeDtypeStruct((t, heads * MLA_V), BF16),
        compiler_params=_cparams(("parallel", "parallel", "arbitrary"), 48),
        name="flash_attn",
    )(i_tab, j_tab, q, k, v)


def _head_sums(x, weight):
    half = 2 * LANES
    gi = lax.broadcasted_iota(jnp.int32, (half, half), 0) // RW_HEAD
    gj = lax.broadcasted_iota(jnp.int32, (half, half), 1) // RW_HEAD
    sel = jnp.where(gi == gj, weight, 0.0).astype(BF16)
    return jnp.concatenate([_dot_sel_rhs(x[:, 0:half], sel, 2), _dot_sel_rhs(x[:, half:], sel, 2)], axis=1)


def _rounds(stages):
    results = [None] * len(stages)
    live = list(range(len(stages)))
    while live:
        for i in list(live):
            try:
                next(stages[i])
            except StopIteration as stop:
                results[i] = stop.value
                live.remove(i)
        yield
    return results


def _lock_step(stages):
    rounds = _rounds(stages)
    while True:
        try:
            next(rounds)
        except StopIteration as stop:
            return stop.value


def _alongside(main, steps, side):
    for _ in range(steps):
        next(main)
        next(side, None)
    for _ in side:
        pass


def _unit_lower_inverse(a, block):
    n = a.shape[0]
    row = lax.broadcasted_iota(jnp.int32, (n, n), 0)
    col = lax.broadcasted_iota(jnp.int32, (n, n), 1)
    diff = row ^ col
    t = jnp.where(row == col, 1.0, 0.0) + jnp.where(diff < 2, a, 0.0)
    size = 2
    while size < block:
        off = jnp.where((diff >= size) & (diff < 2 * size), a, 0.0).astype(BF16)
        tb = t.astype(BF16)
        half = _dot(tb, off).astype(BF16)
        yield
        t = t + _dot(half, tb)
        yield
        size *= 2
    return t


def _rw_chunk_body(has_vres, *refs):
    if has_vres:
        (p_ref, prev_ref, mu_ref, w0_ref, w2_ref, a0_ref, a2_ref, g2_ref, kk_ref, ka_ref, rk_ref,
         v0_ref, v1_ref, v2_ref, vf_ref,
         m_ref, nt_ref, dec_ref, qh_ref, oh_ref, gate_ref, bonus_ref,
         r_s, g_s, k_s, v_s, kk_s, b_s) = refs
    else:
        (p_ref, prev_ref, mu_ref, w0_ref, w2_ref, a0_ref, a2_ref, g2_ref, kk_ref, ka_ref, rk_ref,
         m_ref, nt_ref, dec_ref, qh_ref, oh_ref, gate_ref, bonus_ref, vout_ref,
         r_s, g_s, k_s, v_s, kk_s, b_s) = refs
    c = RW_CHUNK
    w = RW_WIDTH
    bt = p_ref.shape[0]

    p = p_ref[...]
    first = pl.program_id(1) == 0
    prev_row = jnp.where(first, 0.0, prev_ref[7:8, :])
    rowi = lax.broadcasted_iota(jnp.int32, p.shape, 0)
    prev = jnp.where(rowi == 0, prev_row, pltpu.roll(p, 1, 0))
    p = p + (prev - p) * mu_ref[...]
    r = p[:, 0:w]
    k = p[:, w:2 * w]
    v = p[:, 2 * w:3 * w]
    wd = p[:, 3 * w:3 * w + RW_W_RANK]
    ad = p[:, 3 * w + RW_W_RANK:3 * w + RW_W_RANK + RW_A_RANK]
    gd = p[:, 3 * w + RW_W_RANK + RW_A_RANK:]
    w_log = -jax.nn.softplus(-(w0_ref[...] + _dot(jnp.tanh(wd).astype(BF16), w2_ref[...]))) - 0.5
    lr = jax.nn.sigmoid(a0_ref[...] + _dot(ad.astype(BF16), a2_ref[...]))
    gate = _dot(jax.nn.sigmoid(gd).astype(BF16), g2_ref[...])
    if has_vres:
        mix = jax.nn.sigmoid(v0_ref[...] + _dot(_dot(v.astype(BF16), v1_ref[...]).astype(BF16), v2_ref[...]))
        v = v + (vf_ref[...] - v) * mix
    else:
        vout_ref[...] = v
    kk = k * kk_ref[...]
    kk = kk / jnp.maximum(jnp.sqrt(_head_sums(kk * kk, 1.0)), 1e-12)
    k = k * (1.0 + (lr - 1.0) * ka_ref[...])
    gate_ref[...] = gate
    bonus_ref[...] = _head_sums(r * k * rk_ref[...], 1.0) * v
    r_s[...] = r
    g_s[...] = -jnp.exp(w_log)
    k_s[...] = k
    v_s[...] = v
    kk_s[...] = kk
    b_s[...] = kk * lr

    n2 = 2 * c
    row = lax.broadcasted_iota(jnp.int32, (n2, n2), 0)
    col = lax.broadcasted_iota(jnp.int32, (n2, n2), 1)
    same_head = (row ^ col) < c
    incl = same_head & (col <= row)
    strict = same_head & (col < row)
    ci_row = lax.broadcasted_iota(jnp.int32, (c, c), 0)
    ci_col = lax.broadcasted_iota(jnp.int32, (c, c), 1)
    cum = (ci_col <= ci_row).astype(BF16)
    lane = lax.broadcasted_iota(jnp.int32, (1, LANES), 1)
    head0 = lane < RW_HEAD

    def stack(z):
        return jnp.concatenate([jnp.where(head0, z, 0.0), jnp.where(head0, 0.0, z)], axis=0)

    def pair_matrices(g, r, k, v, kkc, bc):
        gc = _dot_sel_lhs(cum, g)
        yield
        g_end = gc[c - 1:c, :]
        inv = jnp.exp(-gc)
        to_end = jnp.exp(g_end - gc)
        a2 = stack(-kkc * jnp.exp(gc - g)).astype(BF16)
        r2 = stack(r * jnp.exp(gc))
        v2 = stack(v).astype(BF16)
        b_end = stack(bc * to_end).astype(BF16)
        k_end = stack(k * to_end).astype(BF16)
        b_inv = (bc * inv).astype(BF16)
        k_inv = (k * inv).astype(BF16)
        ar = jnp.concatenate([a2, r2.astype(BF16)], axis=0)
        abk = _dot_nt(ar, jnp.concatenate([b_inv, b_inv, k_inv, k_inv], axis=0))
        yield
        a_ab = jnp.where(strict, abk[0:n2, 0:n2], 0.0)
        a_rb = jnp.where(incl, abk[n2:2 * n2, 0:n2], 0.0).astype(BF16)
        a_ak = jnp.where(strict, abk[0:n2, n2:2 * n2], 0.0).astype(BF16)
        a_rk = jnp.where(incl, abk[n2:2 * n2, n2:2 * n2], 0.0).astype(BF16)
        x = _dot(a_ak, v2).astype(BF16)
        nt = _dot_tn(v2, k_end)
        oh2 = _dot(a_rk, v2)
        t_inv = (yield from _unit_lower_inverse(a_ab, c)).astype(BF16)
        p12 = _dot(t_inv, jnp.concatenate([a2, x], axis=1)).astype(BF16)
        yield
        qo = _dot(a_rb, p12)
        qh2 = r2 + qo[:, 0:LANES]
        oh2 = oh2 + qo[:, LANES:2 * LANES]
        m = _dot_tn(b_end, p12[:, 0:LANES])
        nt = nt + _dot_tn(p12[:, LANES:2 * LANES], b_end)
        return m, nt, jnp.exp(g_end), qh2[0:c] + qh2[c:n2], oh2[0:c] + oh2[c:n2]

    def chunk(ci, carry):
        units = [(ci * RW_CHUNKS_PER_STEP + u, pr, slice(pr * LANES, (pr + 1) * LANES))
                 for u in range(RW_CHUNKS_PER_STEP) for pr in range(RW_PAIRS)]
        rows = lambda cj: pl.ds(pl.multiple_of(cj * c, c), c)
        loaded = [[s[rows(cj), sl] for s in (g_s, r_s, k_s, v_s, kk_s, b_s)] for cj, _, sl in units]
        results = _lock_step([pair_matrices(*operands) for operands in loaded])
        for (cj, pr, sl), (m, nt, dec, qh, oh) in zip(units, results):
            m_ref[cj, pr] = m.astype(m_ref.dtype)
            nt_ref[cj, pr] = nt
            dec_ref[cj, pr] = dec
            qh_ref[rows(cj), sl] = qh.astype(qh_ref.dtype)
            oh_ref[rows(cj), sl] = oh
        return carry

    lax.fori_loop(0, bt // c // RW_CHUNKS_PER_STEP, chunk, 0)


def rw_chunk(p_rw, mu, w0, w2, a0, a2, g2, k_k, k_a, r_k, vres, *, batch, bt):
    t = p_rw.shape[0]
    nb = t // batch // bt
    nc = bt // RW_CHUNK
    assert t == batch * nb * bt and bt % (RW_CHUNK * RW_CHUNKS_PER_STEP) == 0
    has_vres = vres is not None
    row1 = lambda a: a.reshape(1, -1)
    full = lambda a: pl.BlockSpec(a.shape, lambda b, s: (0,) * a.ndim)
    rows = lambda width: pl.BlockSpec((bt, width), lambda b, s: (b * nb + s, 0))
    prev_spec = pl.BlockSpec((8, RW_IN), lambda b, s: (jnp.maximum((b * nb + s) * (bt // 8) - 1, 0), 0))
    mats = pl.BlockSpec((nc, RW_PAIRS, LANES, LANES), lambda b, s: (b * nb + s, 0, 0, 0))
    params = [row1(mu), row1(w0), w2, row1(a0), a2, g2, row1(k_k), row1(k_a), row1(r_k)]
    args = [p_rw, p_rw] + params
    in_specs = [rows(RW_IN), prev_spec] + [full(a) for a in params]
    if has_vres:
        v0, v1, v2, v_first = vres
        extra = [row1(v0), v1, v2]
        args += extra + [v_first]
        in_specs += [full(a) for a in extra] + [rows(RW_WIDTH)]
    decs = pl.BlockSpec((nc, RW_PAIRS, 1, LANES), lambda b, s: (b * nb + s, 0, 0, 0))
    mat_shape = lambda dt: jax.ShapeDtypeStruct((t // RW_CHUNK, RW_PAIRS, LANES, LANES), dt)
    dec_shape = jax.ShapeDtypeStruct((t // RW_CHUNK, RW_PAIRS, 1, LANES), F32)
    tok_shape = lambda dt: jax.ShapeDtypeStruct((t, RW_WIDTH), dt)
    out_specs = [mats, mats, decs, rows(RW_WIDTH), rows(RW_WIDTH), rows(RW_WIDTH), rows(RW_WIDTH)]
    out_shape = [mat_shape(BF16), mat_shape(F32), dec_shape, tok_shape(BF16), tok_shape(F32), tok_shape(F32),
                 tok_shape(F32)]
    if not has_vres:
        out_specs.append(rows(RW_WIDTH))
        out_shape.append(tok_shape(F32))
    return pl.pallas_call(
        functools.partial(_rw_chunk_body, has_vres),
        grid=(batch, nb),
        in_specs=in_specs,
        out_specs=out_specs,
        out_shape=out_shape,
        scratch_shapes=[pltpu.VMEM((bt, RW_WIDTH), F32) for _ in range(6)],
        compiler_params=_cparams(("parallel", "arbitrary"), 48),
        name="rw_chunk",
    )(*args)


def _rw_scan_body(m_ref, nt_ref, dec_ref, qh_ref, oh_ref, gate_ref, bonus_ref, lnw_ref, lnb_ref, o_ref, y_ref, h_ref):
    c = RW_CHUNK

    @pl.when(pl.program_id(1) == 0)
    def _():
        h_ref[...] = jnp.zeros_like(h_ref)

    def chunk(ci, carry):
        rows = pl.ds(pl.multiple_of(ci * c, c), c)
        for pr in range(RW_PAIRS):
            sl = slice(pr * LANES, (pr + 1) * LANES)
            ht = h_ref[pr]
            hb = ht.astype(BF16)
            y_ref[rows, sl] = _dot_nt(qh_ref[rows, sl], hb) + oh_ref[rows, sl]
            h_ref[pr] = ht * dec_ref[ci, pr] + _dot_nt(hb, m_ref[ci, pr]) + nt_ref[ci, pr]
        return carry

    lax.fori_loop(0, qh_ref.shape[0] // c, chunk, 0)

    y = y_ref[...]
    mean = _head_sums(y, 1.0 / RW_HEAD)
    d = y - mean
    var = _head_sums(d * d, 1.0 / RW_HEAD)
    y = d * lax.rsqrt(var + RW_GN_EPS) * lnw_ref[...] + lnb_ref[...]
    o_ref[...] = ((y + bonus_ref[...]) * gate_ref[...]).astype(o_ref.dtype)


def rw_scan(m, nt, dec, qh, oh, gate, bonus, ln_w, ln_b, *, batch, bs):
    t = qh.shape[0]
    nb = t // batch // bs
    nc = bs // RW_CHUNK
    rows = pl.BlockSpec((bs, RW_WIDTH), lambda b, s: (b * nb + s, 0))
    mats = pl.BlockSpec((nc, RW_PAIRS, LANES, LANES), lambda b, s: (b * nb + s, 0, 0, 0))
    decs = pl.BlockSpec((nc, RW_PAIRS, 1, LANES), lambda b, s: (b * nb + s, 0, 0, 0))
    vec = pl.BlockSpec((1, RW_WIDTH), lambda b, s: (0, 0))
    return pl.pallas_call(
        _rw_scan_body,
        grid=(batch, nb),
        in_specs=[mats, mats, decs, rows, rows, rows, rows, vec, vec],
        out_specs=rows,
        out_shape=jax.ShapeDtypeStruct((t, RW_WIDTH), BF16),
        scratch_shapes=[pltpu.VMEM((bs, RW_WIDTH), F32), pltpu.VMEM((RW_PAIRS, LANES, LANES), F32)],
        compiler_params=_cparams(("parallel", "arbitrary"), 48),
        name="rw_scan",
    )(m, nt, dec, qh, oh, gate, bonus, ln_w.reshape(1, -1), ln_b.reshape(1, -1))


def _rope_tables(positions):
    half = MLA_ROPE // 2
    inv_freq = ROPE_THETA ** (-jnp.arange(half, dtype=F32) / half)
    ang = positions.astype(F32).reshape(-1, 1) * inv_freq
    zeros = jnp.zeros((ang.shape[0], LANES - MLA_ROPE), F32)
    cos = jnp.concatenate([jnp.cos(ang), jnp.cos(ang), zeros], axis=1)
    sin = jnp.concatenate([jnp.sin(ang), jnp.sin(ang), zeros], axis=1)
    return cos, sin


def _pad_cols(a, width):
    return jnp.pad(a, [(0, 0)] * (a.ndim - 1) + [(0, width - a.shape[-1])])


def _forward(x, positions, attn_norm, w_in, hg_lower_bounds, hg_out_norm,
             mla_q_a_norm, mla_w_qb, mla_kv_a_norm, mla_w_kvb, mla_q_norm, mla_k_norm,
             rw_mu, rw_w0, rw_w2, rw_a0, rw_a2, rw_g2, rw_v0, rw_v1, rw_v2,
             rw_k_k, rw_k_a, rw_r_k, rw_ln_w, rw_ln_b,
             w_o, ffn_norm, w_gate_up, w_down, *, cfg):
    batch, seq, d = x.shape
    depth = w_in.shape[0]
    t = batch * seq
    x = x.reshape(t, d)

    lb_sm = jax.nn.softmax(hg_lower_bounds.astype(F32), axis=0)
    lower_bounds = jnp.cumsum(lb_sm, axis=0) - lb_sm[0]
    cos, sin = _rope_tables(positions)

    hg_end = 4 * HG_WIDTH
    mla_end = hg_end + MLA_IN
    w_in_hg = w_in[:, :, :hg_end].astype(BF16)
    w_in_mla = _pad_cols(w_in[:, :, hg_end:mla_end], MLA_IN_PAD).astype(BF16)
    w_in_rw = w_in[:, :, mla_end:].astype(BF16)
    w_qb = _pad_cols(mla_w_qb.reshape(depth, MLA_Q_RANK, MLA_HEADS, MLA_QK_DIM), MLA_QK_PAD)
    w_qb = w_qb.reshape(depth, MLA_Q_RANK, MLA_HEADS * MLA_QK_PAD).astype(BF16)
    w_kvb = mla_w_kvb.astype(BF16)
    split_gain = lambda g: jnp.stack([g[:, :MLA_NOPE], _pad_cols(g[:, MLA_NOPE:], LANES)], axis=1)
    g_q = split_gain(mla_q_norm)
    g_k = split_gain(mla_k_norm)
    w_o_hg = w_o[:, :HG_WIDTH].astype(BF16)
    w_o_mla = w_o[:, HG_WIDTH:HG_WIDTH + MLA_WIDTH].astype(BF16)
    w_o_rw = w_o[:, HG_WIDTH + MLA_WIDTH:].astype(BF16)
    w_gu = w_gate_up.astype(BF16)
    w_dn = w_down.astype(BF16)
    rw_w2b, rw_a2b, rw_g2b = rw_w2.astype(BF16), rw_a2.astype(BF16), rw_g2.astype(BF16)
    rw_v1b, rw_v2b = rw_v1.astype(BF16), rw_v2.astype(BF16)
    r_k = rw_r_k.reshape(depth, RW_WIDTH)

    v_first = None
    for l in range(depth):
        p_rw, q, k, v, o_hg = mixer_in(x, attn_norm[l], w_in_hg[l], w_in_mla[l], w_in_rw[l], cos, sin,
                                       mla_q_a_norm[l], w_qb[l], mla_kv_a_norm[l], w_kvb[l], g_q[l], g_k[l],
                                       lower_bounds[l], hg_out_norm[l], batch=batch, bm=cfg["proj_bm"])
        o_mla = flash_attn(q, k, v, batch=batch, blk=cfg["attn_blk"])

        vres = None if l == 0 else (rw_v0[l - 1], rw_v1b[l - 1], rw_v2b[l - 1], v_first)
        outs = rw_chunk(p_rw, rw_mu[l], rw_w0[l], rw_w2b[l], rw_a0[l], rw_a2b[l], rw_g2b[l],
                        rw_k_k[l], rw_k_a[l], r_k[l], vres, batch=batch, bt=cfg["rw_bt"])
        if l == 0:
            v_first = outs[7]
        o_rw = rw_scan(*outs[:7], rw_ln_w[l], rw_ln_b[l], batch=batch, bs=cfg["rw_bs"])

        x = out_proj(x, o_hg, o_mla, o_rw, w_o_hg[l], w_o_mla[l], w_o_rw[l], bm=cfg["out_bm"])
        x = ffn(x, ffn_norm[l], w_gu[l], w_dn[l], bm=cfg["ffn_bm"], bf=cfg["ffn_bf"])
    return x.reshape(batch, seq, d)


_CFG = dict(proj_bm=256, attn_blk=1024, rw_bt=256, rw_bs=512,
            out_bm=512, ffn_bm=1024, ffn_bf=512)


def kernel(x, positions, attn_norm, w_in, hg_lower_bounds, hg_out_norm, mla_q_a_norm, mla_w_qb, mla_kv_a_norm,
           mla_w_kvb, mla_q_norm, mla_k_norm, rw_mu, rw_w0, rw_w2, rw_a0, rw_a2, rw_g2, rw_v0, rw_v1, rw_v2,
           rw_k_k, rw_k_a, rw_r_k, rw_ln_w, rw_ln_b, w_o, ffn_norm, w_gate_up, w_down):
    return _forward(x, positions, attn_norm, w_in, hg_lower_bounds, hg_out_norm, mla_q_a_norm, mla_w_qb,
                    mla_kv_a_norm, mla_w_kvb, mla_q_norm, mla_k_norm, rw_mu, rw_w0, rw_w2, rw_a0, rw_a2, rw_g2,
                    rw_v0, rw_v1, rw_v2, rw_k_k, rw_k_a, rw_r_k, rw_ln_w, rw_ln_b, w_o, ffn_norm, w_gate_up,
                    w_down, cfg=_CFG)
```

```python
import functools

import jax
import jax.numpy as jnp
from jax import lax
from jax.experimental import pallas as pl
from jax.experimental.pallas import tpu as pltpu

F32 = jnp.float32
BF16 = jnp.bfloat16

D_MODEL = 2048
DEPTH = 4
HG_WIDTH = 512
HG_HEAD_DIM = 128
HG_HEADS = HG_WIDTH // HG_HEAD_DIM
MLA_NOPE = 128
MLA_ROPE = 64
MLA_V = 128
MLA_HEADS = 8
MLA_WIDTH = MLA_HEADS * MLA_V
MLA_Q_RANK = 512
MLA_KV_RANK = 256
MLA_QK_DIM = MLA_NOPE + MLA_ROPE
MLA_QK_PAD = 256
MLA_IN = MLA_Q_RANK + MLA_KV_RANK + MLA_ROPE
MLA_IN_PAD = 896
ROPE_THETA = 10000.0
MASK_VALUE = -1e30
RW_WIDTH = 512
RW_HEAD = 64
RW_HEADS = RW_WIDTH // RW_HEAD
RW_PAIRS = RW_HEADS // 2
RW_W_RANK = 64
RW_A_RANK = 64
RW_V_RANK = 32
RW_G_RANK = 128
RW_IN = 3 * RW_WIDTH + RW_W_RANK + RW_A_RANK + RW_G_RANK
RW_GN_EPS = 64e-5
MIX_WIDTH = HG_WIDTH + MLA_WIDTH + RW_WIDTH
D_FF = 5632
RMS_EPS = 1e-6

HG_CHUNK = 32
HG_EXP_CLAMP = 80.0
HG_CHUNKS_PER_STEP = 4
RW_CHUNK = 64
RW_CHUNKS_PER_STEP = 4
ATTN_ROW_CHAINS = 4
PROJ_COLS = 512
OUT_COLS = 256
LANES = 128

_MIB = 1024 * 1024


def _cparams(semantics, vmem_mib):
    return pltpu.CompilerParams(dimension_semantics=semantics, vmem_limit_bytes=vmem_mib * _MIB)


def _dot(a, b):
    return jnp.dot(a, b, preferred_element_type=F32)


def _dot_nt(a, b):
    return lax.dot_general(a, b, (((1,), (1,)), ((), ())), preferred_element_type=F32)


def _dot_tn(a, b):
    return lax.dot_general(a, b, (((0,), (0,)), ((), ())), preferred_element_type=F32)


def _split(x, terms):
    parts = []
    for _ in range(terms - 1):
        hi = x.astype(BF16)
        parts.append(hi)
        x = x - hi.astype(F32)
    parts.append(x.astype(BF16))
    return parts


def _dot_sel_lhs(sel, x, terms=3):
    return sum(_dot(sel, part) for part in _split(x, terms))


def _dot_sel_rhs(x, sel, terms=3):
    return sum(_dot(part, sel) for part in _split(x, terms))


def _rms(x, width):
    ms = jnp.sum(x * x, axis=-1, keepdims=True) * (1.0 / width)
    return x * lax.rsqrt(ms + RMS_EPS)


def _ffn_body(x_ref, g_ref, wg_ref, wu_ref, wd_ref, o_ref, h_ref):
    @pl.when(pl.program_id(1) == 0)
    def _():
        x = x_ref[...]
        h_ref[...] = (_rms(x, x.shape[-1]) * g_ref[...]).astype(BF16)
        o_ref[...] = x

    h = h_ref[...]
    gate = _dot(h, wg_ref[...])
    up = _dot(h, wu_ref[...])
    act = (gate * jax.nn.sigmoid(gate) * up).astype(BF16)
    o_ref[...] += _dot(act, wd_ref[...])


def ffn(x, gain, w_gate_up, w_down, *, bm, bf):
    t, d = x.shape
    dff = w_down.shape[0]
    nf = dff // bf
    return pl.pallas_call(
        _ffn_body,
        grid=(t // bm, nf),
        in_specs=[
            pl.BlockSpec((bm, d), lambda i, j: (i, 0)),
            pl.BlockSpec((1, d), lambda i, j: (0, 0)),
            pl.BlockSpec((d, bf), lambda i, j: (0, j)),
            pl.BlockSpec((d, bf), lambda i, j: (0, j + nf)),
            pl.BlockSpec((bf, d), lambda i, j: (j, 0)),
        ],
        out_specs=pl.BlockSpec((bm, d), lambda i, j: (i, 0)),
        out_shape=jax.ShapeDtypeStruct((t, d), F32),
        scratch_shapes=[pltpu.VMEM((bm, d), BF16)],
        compiler_params=_cparams(("parallel", "arbitrary"), 56),
        name="ffn",
    )(x, gain.reshape(1, d), w_gate_up, w_gate_up, w_down)


def _hgrn2_stages(p_ref, lb_ref, gain_ref, o_ref, st_ref, first_chunk):
    c = HG_CHUNK
    w = HG_WIDTH
    row = lax.broadcasted_iota(jnp.int32, (c, c), 0)
    col = lax.broadcasted_iota(jnp.int32, (c, c), 1)
    tril = col <= row
    cum = tril.astype(BF16)
    lb = lb_ref[...]
    gain = gain_ref[...]
    heads = [slice(h * HG_HEAD_DIM, (h + 1) * HG_HEAD_DIM) for h in range(HG_HEADS)]
    states = [[st_ref[h] for h in range(HG_HEADS)]] + [None] * HG_CHUNKS_PER_STEP

    def chunk_stages(rows, u):
        q_raw = p_ref[rows, 0:w]
        z = p_ref[rows, w:2 * w]
        v = p_ref[rows, 2 * w:3 * w]
        g_raw = p_ref[rows, 3 * w:4 * w]
        q = q_raw * jax.nn.sigmoid(q_raw)
        log_f = jnp.log(lb + (1.0 - lb) * jax.nn.sigmoid(z))
        k = (1.0 - lb) * jax.nn.sigmoid(-z)
        b = _dot_sel_lhs(cum, log_f)
        yield
        b_mid = b[c // 2 - 1:c // 2, :]
        b_end = b[c - 1:c, :]
        q_in = (q * jnp.exp(jnp.minimum(b - b_mid, HG_EXP_CLAMP))).astype(BF16)
        k_in = (k * jnp.exp(jnp.minimum(b_mid - b, HG_EXP_CLAMP))).astype(BF16)
        q_st = (q * jnp.exp(b)).astype(BF16)
        k_st = (k * jnp.exp(b_end - b)).astype(BF16)
        d_end = jnp.exp(b_end)
        vb = v.astype(BF16)
        gate = g_raw * jax.nn.sigmoid(g_raw)
        scores = [jnp.where(tril, _dot_nt(q_in[:, sl], k_in[:, sl]), 0.0).astype(BF16) for sl in heads]
        updates = [_dot_tn(vb[:, sl], k_st[:, sl]) for sl in heads]
        yield
        st = states[u]
        states[u + 1] = [st[h] * d_end[:, sl] + updates[h] for h, sl in enumerate(heads)]
        outs = [_dot_nt(q_st[:, sl], st[h].astype(BF16)) + _dot(scores[h], vb[:, sl]) for h, sl in enumerate(heads)]
        yield
        for h, sl in enumerate(heads):
            o_ref[rows, sl] = (_rms(outs[h], HG_HEAD_DIM) * gain * gate[:, sl]).astype(o_ref.dtype)

    yield from _rounds([chunk_stages(pl.ds(pl.multiple_of((first_chunk + u) * c, c), c), u)
                        for u in range(HG_CHUNKS_PER_STEP)])
    for h in range(HG_HEADS):
        st_ref[h] = states[HG_CHUNKS_PER_STEP][h]


def _rope(x, cos, sin):
    half = MLA_ROPE // 2
    lane = lax.broadcasted_iota(jnp.int32, x.shape, 1)
    rot = jnp.where(lane < half, -pltpu.roll(x, LANES - half, 1), pltpu.roll(x, half, 1))
    return x * cos + rot * sin


def _mla_stages(p_ref, cos_ref, sin_ref, gqa_ref, wqb_ref, gkva_ref, wkvb_ref, gq_ref, gk_ref, q_ref, k_ref, v_ref):
    p = p_ref[...]
    cos = cos_ref[...]
    sin = sin_ref[...]
    scale = MLA_QK_DIM ** -0.5
    qn = (_rms(p[:, 0:MLA_Q_RANK], MLA_Q_RANK) * gqa_ref[...]).astype(BF16)
    q = _dot(qn, wqb_ref[...])
    kvn = (_rms(p[:, MLA_Q_RANK:MLA_Q_RANK + MLA_KV_RANK], MLA_KV_RANK) * gkva_ref[...]).astype(BF16)
    kv = _dot(kvn, wkvb_ref[...])
    gq_nope, gq_rope = gq_ref[0:1, :], gq_ref[1:2, :]
    gk_nope, gk_rope = gk_ref[0:1, :], gk_ref[1:2, :]
    k_rope = _rope(_rms(p[:, MLA_Q_RANK + MLA_KV_RANK:], MLA_ROPE) * gk_rope, cos, sin).astype(BF16)
    half = MLA_ROPE // 2
    first_half = lax.broadcasted_iota(jnp.int32, cos.shape, 1) < half
    yield

    def head_stages(h):
        o = h * MLA_QK_PAD
        q_nope, q_rope, k_nope = q[:, o:o + MLA_NOPE], q[:, o + MLA_NOPE:o + MLA_QK_PAD], kv[:, o:o + MLA_NOPE]
        ss_qn = jnp.sum(q_nope * q_nope, axis=-1, keepdims=True)
        ss_qr = jnp.sum(q_rope * q_rope, axis=-1, keepdims=True)
        ss_kn = jnp.sum(k_nope * k_nope, axis=-1, keepdims=True)
        yield
        q_nope = q_nope * lax.rsqrt(ss_qn * (1.0 / MLA_NOPE) + RMS_EPS) * (gq_nope * scale)
        q_rope = q_rope * lax.rsqrt(ss_qr * (1.0 / MLA_ROPE) + RMS_EPS) * (gq_rope * scale)
        k_nope = k_nope * lax.rsqrt(ss_kn * (1.0 / MLA_NOPE) + RMS_EPS) * gk_nope
        up, down = pltpu.roll(q_rope, LANES - half, 1), pltpu.roll(q_rope, half, 1)
        yield
        q_ref[h, :, 0:MLA_NOPE] = q_nope.astype(BF16)
        q_ref[h, :, MLA_NOPE:MLA_QK_PAD] = (q_rope * cos + jnp.where(first_half, -up, down) * sin).astype(BF16)
        k_ref[h, :, 0:MLA_NOPE] = k_nope.astype(BF16)
        k_ref[h, :, MLA_NOPE:MLA_QK_PAD] = k_rope
        v_ref[h] = kv[:, o + MLA_NOPE:o + MLA_QK_PAD].astype(BF16)

    yield from _rounds([head_stages(h) for h in range(MLA_HEADS)])


def _mixer_in_body(blocks_per_seq, x_ref, g_ref, whg_ref, wmla_ref, wrw_ref,
                   cos_ref, sin_ref, gqa_ref, wqb_ref, gkva_ref, wkvb_ref, gq_ref, gk_ref, lb_ref, hgain_ref,
                   prw_ref, q_ref, k_ref, v_ref, ohg_ref,
                   h_s, phg_s, pmla_s, st_ref):
    @pl.when(pl.program_id(0) % blocks_per_seq == 0)
    def _():
        st_ref[...] = jnp.zeros_like(st_ref)

    h_s[...] = (_rms(x_ref[...], x_ref.shape[-1]) * g_ref[...]).astype(BF16)

    def projections():
        pmla_s[...] = _dot(h_s[...], wmla_ref[...])
        yield
        for c0 in range(0, whg_ref.shape[1], PROJ_COLS):
            phg_s[:, c0:c0 + PROJ_COLS] = _dot(h_s[...], whg_ref[:, c0:c0 + PROJ_COLS])
            yield
        for c0 in range(0, wrw_ref.shape[1], PROJ_COLS):
            c1 = min(c0 + PROJ_COLS, wrw_ref.shape[1])
            prw_ref[:, c0:c1] = _dot(h_s[...], wrw_ref[:, c0:c1])
            yield

    def hgrn2_blocks():
        for first in range(0, x_ref.shape[0] // HG_CHUNK, HG_CHUNKS_PER_STEP):
            yield from _hgrn2_stages(phg_s, lb_ref, hgain_ref, ohg_ref, st_ref, first)

    proj = projections()
    next(proj)
    _alongside(proj, whg_ref.shape[1] // PROJ_COLS,
               _mla_stages(pmla_s, cos_ref, sin_ref, gqa_ref, wqb_ref, gkva_ref, wkvb_ref, gq_ref, gk_ref,
                           q_ref, k_ref, v_ref))
    _alongside(proj, pl.cdiv(wrw_ref.shape[1], PROJ_COLS), hgrn2_blocks())


def mixer_in(x, gain, w_hg, w_mla, w_rw, cos, sin, g_qa, w_qb, g_kva, w_kvb, g_q, g_k, lower_bound, out_gain,
             *, batch, bm):
    t, d = x.shape
    assert t % (batch * bm) == 0 and bm % (HG_CHUNK * HG_CHUNKS_PER_STEP) == 0
    resident = lambda a: pl.BlockSpec(a.shape, lambda i: (0,) * a.ndim, pipeline_mode=pl.Buffered(1))
    rows = lambda width: pl.BlockSpec((bm, width), lambda i: (i, 0))
    heads = lambda width: pl.BlockSpec((MLA_HEADS, bm, width), lambda i: (0, i, 0))
    params = [gain.reshape(1, d), w_hg, w_mla, w_rw]
    mla_params = [g_qa.reshape(1, -1), w_qb, g_kva.reshape(1, -1), w_kvb, g_q, g_k]
    hg_params = [lower_bound.reshape(1, HG_WIDTH), out_gain.reshape(1, HG_HEAD_DIM)]
    return pl.pallas_call(
        functools.partial(_mixer_in_body, t // batch // bm),
        grid=(t // bm,),
        in_specs=([rows(d)] + [resident(a) for a in params] + [rows(LANES), rows(LANES)]
                  + [resident(a) for a in mla_params + hg_params]),
        out_specs=[rows(w_rw.shape[1]), heads(MLA_QK_PAD), heads(MLA_QK_PAD), heads(MLA_V), rows(HG_WIDTH)],
        out_shape=[jax.ShapeDtypeStruct((t, w_rw.shape[1]), F32),
                   jax.ShapeDtypeStruct((MLA_HEADS, t, MLA_QK_PAD), BF16),
                   jax.ShapeDtypeStruct((MLA_HEADS, t, MLA_QK_PAD), BF16),
                   jax.ShapeDtypeStruct((MLA_HEADS, t, MLA_V), BF16),
                   jax.ShapeDtypeStruct((t, HG_WIDTH), BF16)],
        scratch_shapes=[pltpu.VMEM((bm, d), BF16), pltpu.VMEM((bm, w_hg.shape[1]), F32),
                        pltpu.VMEM((bm, w_mla.shape[1]), F32),
                        pltpu.VMEM((HG_HEADS, HG_HEAD_DIM, HG_HEAD_DIM), F32)],
        compiler_params=_cparams(("arbitrary",), 56),
        name="mixer_in",
    )(x, *params, cos, sin, *mla_params, *hg_params)


def _pipelined(stages):
    n = len(stages)
    done = [False] * n
    rnd = 0
    while not all(done):
        for i in range(min(rnd, n - 1) + 1):
            if not done[i]:
                try:
                    next(stages[i])
                except StopIteration:
                    done[i] = True
        rnd += 1


def _flash_body(i_tab, j_tab, q_ref, k_ref, v_ref, o_ref, m_ref, l_ref, acc_ref):
    pair = pl.program_id(2)
    i = i_tab[pair]
    j = j_tab[pair]
    blk = q_ref.shape[1]
    rb = blk // ATTN_ROW_CHAINS

    @pl.when(j == 0)
    def _():
        m_ref[...] = jnp.full_like(m_ref, MASK_VALUE)
        l_ref[...] = jnp.zeros_like(l_ref)
        acc_ref[...] = jnp.zeros_like(acc_ref)

    def row_group(r, diagonal):
        rows = slice(r * rb, (r + 1) * rb)
        ncol = (r + 1) * rb if diagonal else blk
        s = _dot_nt(q_ref[0, rows, :], k_ref[0, 0:ncol, :])
        yield
        if diagonal:
            row = lax.broadcasted_iota(jnp.int32, (rb, rb), 0)
            col = lax.broadcasted_iota(jnp.int32, (rb, rb), 1)
            last = jnp.where(col <= row, s[:, r * rb:], MASK_VALUE)
            s = jnp.concatenate([s[:, :r * rb], last], axis=1) if r else last
        m_old = m_ref[rows, :]
        m_new = jnp.maximum(m_old, jnp.max(s, axis=-1, keepdims=True))
        alpha = jnp.exp(m_old - m_new)
        p = jnp.exp(s - jnp.concatenate([m_new] * (ncol // LANES), axis=1))
        l_new = alpha * l_ref[rows, :] + jnp.sum(p, axis=-1, keepdims=True)
        pb = p.astype(BF16)
        yield
        acc = alpha * acc_ref[rows, :] + _dot(pb, v_ref[0, 0:ncol, :])
        if diagonal:
            o_ref[rows, :] = (acc / l_new).astype(o_ref.dtype)
        else:
            m_ref[rows, :] = m_new
            l_ref[rows, :] = l_new
            acc_ref[rows, :] = acc

    @pl.when(j < i)
    def _():
        _pipelined([row_group(r, False) for r in range(ATTN_ROW_CHAINS)])

    @pl.when(j == i)
    def _():
        _pipelined([row_group(r, True) for r in range(ATTN_ROW_CHAINS)])


def flash_attn(q, k, v, *, batch, blk):
    heads, t, _ = q.shape
    nb = t // batch // blk
    assert t == batch * nb * blk and blk % (ATTN_ROW_CHAINS * LANES) == 0
    pairs =[(i, j) for i in range(nb) for j in range(i + 1)]
    i_tab = jnp.array([p[0] for p in pairs], jnp.int32)
    j_tab = jnp.array([p[1] for p in pairs], jnp.int32)
    qmap = lambda b, h, p, it, jt: (h, b * nb + it[p], 0)
    kmap = lambda b, h, p, it, jt: (h, b * nb + jt[p], 0)
    return pl.pallas_call(
        _flash_body,
        grid_spec=pltpu.PrefetchScalarGridSpec(
            num_scalar_prefetch=2,
            grid=(batch, heads, len(pairs)),
            in_specs=[pl.BlockSpec((1, blk, MLA_QK_PAD), qmap),
                      pl.BlockSpec((1, blk, MLA_QK_PAD), kmap),
                      pl.BlockSpec((1, blk, MLA_V), kmap)],
            out_specs=pl.BlockSpec((blk, MLA_V), lambda b, h, p, it, jt: (b * nb + it[p], h)),
            scratch_shapes=[pltpu.VMEM((blk, LANES), F32), pltpu.VMEM((blk, LANES), F32),
                            pltpu.VMEM((blk, MLA_V), F32)],
        ),
        out_shape=jax.ShapeDtypeStruct((t, heads * MLA_V), BF16),
        compiler_params=_cparams(("parallel", "parallel", "arbitrary"), 48),
        name="flash_attn",
    )(i_tab, j_tab, q, k, v)


def _head_sums(x, weight):
    half = 2 * LANES
    gi = lax.broadcasted_iota(jnp.int32, (half, half), 0) // RW_HEAD
    gj = lax.broadcasted_iota(jnp.int32, (half, half), 1) // RW_HEAD
    sel = jnp.where(gi == gj, weight, 0.0).astype(BF16)
    return jnp.concatenate([_dot_sel_rhs(x[:, 0:half], sel, 2), _dot_sel_rhs(x[:, half:], sel, 2)], axis=1)


def _rounds(stages):
    results = [None] * len(stages)
    live = list(range(len(stages)))
    while live:
        for i in list(live):
            try:
                next(stages[i])
            except StopIteration as stop:
                results[i] = stop.value
                live.remove(i)
        yield
    return results


def _lock_step(stages):
    rounds = _rounds(stages)
    while True:
        try:
            next(rounds)
        except StopIteration as stop:
            return stop.value


def _alongside(main, steps, side):
    for _ in range(steps):
        next(main)
        next(side, None)
    for _ in side:
        pass


def _unit_lower_inverse(a, block):
    n = a.shape[0]
    row = lax.broadcasted_iota(jnp.int32, (n, n), 0)
    col = lax.broadcasted_iota(jnp.int32, (n, n), 1)
    diff = row ^ col
    t = jnp.where(row == col, 1.0, 0.0) + jnp.where(diff < 2, a, 0.0)
    size = 2
    while size < block:
        off = jnp.where((diff >= size) & (diff < 2 * size), a, 0.0).astype(BF16)
        tb = t.astype(BF16)
        half = _dot(tb, off).astype(BF16)
        yield
        t = t + _dot(half, tb)
        yield
        size *= 2
    return t


def _rw_chunk_body(has_vres, *refs):
    if has_vres:
        (p_ref, prev_ref, mu_ref, w0_ref, w2_ref, a0_ref, a2_ref, g2_ref, kk_ref, ka_ref, rk_ref,
         v0_ref, v1_ref, v2_ref, vf_ref,
         m_ref, nt_ref, dec_ref, qh_ref, oh_ref, gate_ref, bonus_ref,
         r_s, g_s, k_s, v_s, kk_s, b_s) = refs
    else:
        (p_ref, prev_ref, mu_ref, w0_ref, w2_ref, a0_ref, a2_ref, g2_ref, kk_ref, ka_ref, rk_ref,
         m_ref, nt_ref, dec_ref, qh_ref, oh_ref, gate_ref, bonus_ref, vout_ref,
         r_s, g_s, k_s, v_s, kk_s, b_s) = refs
    c = RW_CHUNK
    w = RW_WIDTH
    bt = p_ref.shape[0]

    p = p_ref[...]
    first = pl.program_id(1) == 0
    prev_row = jnp.where(first, 0.0, prev_ref[7:8, :])
    rowi = lax.broadcasted_iota(jnp.int32, p.shape, 0)
    prev = jnp.where(rowi == 0, prev_row, pltpu.roll(p, 1, 0))
    p = p + (prev - p) * mu_ref[...]
    r = p[:, 0:w]
    k = p[:, w:2 * w]
    v = p[:, 2 * w:3 * w]
    wd = p[:, 3 * w:3 * w + RW_W_RANK]
    ad = p[:, 3 * w + RW_W_RANK:3 * w + RW_W_RANK + RW_A_RANK]
    gd = p[:, 3 * w + RW_W_RANK + RW_A_RANK:]
    w_log = -jax.nn.softplus(-(w0_ref[...] + _dot(jnp.tanh(wd).astype(BF16), w2_ref[...]))) - 0.5
    lr = jax.nn.sigmoid(a0_ref[...] + _dot(ad.astype(BF16), a2_ref[...]))
    gate = _dot(jax.nn.sigmoid(gd).astype(BF16), g2_ref[...])
    if has_vres:
        mix = jax.nn.sigmoid(v0_ref[...] + _dot(_dot(v.astype(BF16), v1_ref[...]).astype(BF16), v2_ref[...]))
        v = v + (vf_ref[...] - v) * mix
    else:
        vout_ref[...] = v
    kk = k * kk_ref[...]
    kk = kk / jnp.maximum(jnp.sqrt(_head_sums(kk * kk, 1.0)), 1e-12)
    k = k * (1.0 + (lr - 1.0) * ka_ref[...])
    gate_ref[...] = gate
    bonus_ref[...] = _head_sums(r * k * rk_ref[...], 1.0) * v
    r_s[...] = r
    g_s[...] = -jnp.exp(w_log)
    k_s[...] = k
    v_s[...] = v
    kk_s[...] = kk
    b_s[...] = kk * lr

    n2 = 2 * c
    row = lax.broadcasted_iota(jnp.int32, (n2, n2), 0)
    col = lax.broadcasted_iota(jnp.int32, (n2, n2), 1)
    same_head = (row ^ col) < c
    incl = same_head & (col <= row)
    strict = same_head & (col < row)
    ci_row = lax.broadcasted_iota(jnp.int32, (c, c), 0)
    ci_col = lax.broadcasted_iota(jnp.int32, (c, c), 1)
    cum = (ci_col <= ci_row).astype(BF16)
    lane = lax.broadcasted_iota(jnp.int32, (1, LANES), 1)
    head0 = lane < RW_HEAD

    def stack(z):
        return jnp.concatenate([jnp.where(head0, z, 0.0), jnp.where(head0, 0.0, z)], axis=0)

    def pair_matrices(g, r, k, v, kkc, bc):
        gc = _dot_sel_lhs(cum, g)
        yield
        g_end = gc[c - 1:c, :]
        inv = jnp.exp(-gc)
        to_end = jnp.exp(g_end - gc)
        a2 = stack(-kkc * jnp.exp(gc - g)).astype(BF16)
        r2 = stack(r * jnp.exp(gc))
        v2 = stack(v).astype(BF16)
        b_end = stack(bc * to_end).astype(BF16)
        k_end = stack(k * to_end).astype(BF16)
        b_inv = (bc * inv).astype(BF16)
        k_inv = (k * inv).astype(BF16)
        ar = jnp.concatenate([a2, r2.astype(BF16)], axis=0)
        abk = _dot_nt(ar, jnp.concatenate([b_inv, b_inv, k_inv, k_inv], axis=0))
        yield
        a_ab = jnp.where(strict, abk[0:n2, 0:n2], 0.0)
        a_rb = jnp.where(incl, abk[n2:2 * n2, 0:n2], 0.0).astype(BF16)
        a_ak = jnp.where(strict, abk[0:n2, n2:2 * n2], 0.0).astype(BF16)
        a_rk = jnp.where(incl, abk[n2:2 * n2, n2:2 * n2], 0.0).astype(BF16)
        x = _dot(a_ak, v2).astype(BF16)
        nt = _dot_tn(v2, k_end)
        oh2 = _dot(a_rk, v2)
        t_inv = (yield from _unit_lower_inverse(a_ab, c)).astype(BF16)
        p12 = _dot(t_inv, jnp.concatenate([a2, x], axis=1)).astype(BF16)
        yield
        qo = _dot(a_rb, p12)
        qh2 = r2 + qo[:, 0:LANES]
        oh2 = oh2 + qo[:, LANES:2 * LANES]
        m = _dot_tn(b_end, p12[:, 0:LANES])
        nt = nt + _dot_tn(p12[:, LANES:2 * LANES], b_end)
        return m, nt, jnp.exp(g_end), qh2[0:c] + qh2[c:n2], oh2[0:c] + oh2[c:n2]

    def chunk(ci, carry):
        units = [(ci * RW_CHUNKS_PER_STEP + u, pr, slice(pr * LANES, (pr + 1) * LANES))
                 for u in range(RW_CHUNKS_PER_STEP) for pr in range(RW_PAIRS)]
        rows = lambda cj: pl.ds(pl.multiple_of(cj * c, c), c)
        loaded = [[s[rows(cj), sl] for s in (g_s, r_s, k_s, v_s, kk_s, b_s)] for cj, _, sl in units]
        results = _lock_step([pair_matrices(*operands) for operands in loaded])
        for (cj, pr, sl), (m, nt, dec, qh, oh) in zip(units, results):
            m_ref[cj, pr] = m.astype(m_ref.dtype)
            nt_ref[cj, pr] = nt
            dec_ref[cj, pr] = dec
            qh_ref[rows(cj), sl] = qh.astype(qh_ref.dtype)
            oh_ref[rows(cj), sl] = oh
        return carry

    lax.fori_loop(0, bt // c // RW_CHUNKS_PER_STEP, chunk, 0)


def rw_chunk(p_rw, mu, w0, w2, a0, a2, g2, k_k, k_a, r_k, vres, *, batch, bt):
    t = p_rw.shape[0]
    nb = t // batch // bt
    nc = bt // RW_CHUNK
    assert t == batch * nb * bt and bt % (RW_CHUNK * RW_CHUNKS_PER_STEP) == 0
    has_vres = vres is not None
    row1 = lambda a: a.reshape(1, -1)
    full = lambda a: pl.BlockSpec(a.shape, lambda b, s: (0,) * a.ndim)
    rows = lambda width: pl.BlockSpec((bt, width), lambda b, s: (b * nb + s, 0))
    prev_spec = pl.BlockSpec((8, RW_IN), lambda b, s: (jnp.maximum((b * nb + s) * (bt // 8) - 1, 0), 0))
    mats = pl.BlockSpec((nc, RW_PAIRS, LANES, LANES), lambda b, s: (b * nb + s, 0, 0, 0))
    params = [row1(mu), row1(w0), w2, row1(a0), a2, g2, row1(k_k), row1(k_a), row1(r_k)]
    args = [p_rw, p_rw] + params
    in_specs = [rows(RW_IN), prev_spec] + [full(a) for a in params]
    if has_vres:
        v0, v1, v2, v_first = vres
        extra = [row1(v0), v1, v2]
        args += extra + [v_first]
        in_specs += [full(a) for a in extra] + [rows(RW_WIDTH)]
    decs = pl.BlockSpec((nc, RW_PAIRS, 1, LANES), lambda b, s: (b * nb + s, 0, 0, 0))
    mat_shape = lambda dt: jax.ShapeDtypeStruct((t // RW_CHUNK, RW_PAIRS, LANES, LANES), dt)
    dec_shape = jax.ShapeDtypeStruct((t // RW_CHUNK, RW_PAIRS, 1, LANES), F32)
    tok_shape = lambda dt: jax.ShapeDtypeStruct((t, RW_WIDTH), dt)
    out_specs = [mats, mats, decs, rows(RW_WIDTH), rows(RW_WIDTH), rows(RW_WIDTH), rows(RW_WIDTH)]
    out_shape = [mat_shape(BF16), mat_shape(F32), dec_shape, tok_shape(BF16), tok_shape(F32), tok_shape(F32),
                 tok_shape(F32)]
    if not has_vres:
        out_specs.append(rows(RW_WIDTH))
        out_shape.append(tok_shape(F32))
    return pl.pallas_call(
        functools.partial(_rw_chunk_body, has_vres),
        grid=(batch, nb),
        in_specs=in_specs,
        out_specs=out_specs,
        out_shape=out_shape,
        scratch_shapes=[pltpu.VMEM((bt, RW_WIDTH), F32) for _ in range(6)],
        compiler_params=_cparams(("parallel", "arbitrary"), 48),
        name="rw_chunk",
    )(*args)


def _rw_scan_stages(m_ref, nt_ref, dec_ref, qh_ref, oh_ref, gate_ref, bonus_ref, lnw_ref, lnb_ref, o_ref, y_ref, h_ref):
    c = RW_CHUNK
    states = [h_ref[pr] for pr in range(RW_PAIRS)]
    for ci in range(qh_ref.shape[0] // c):
        rows = slice(ci * c, (ci + 1) * c)
        for pr in range(RW_PAIRS):
            sl = slice(pr * LANES, (pr + 1) * LANES)
            ht = states[pr]
            hb = ht.astype(BF16)
            y_ref[rows, sl] = _dot_nt(qh_ref[rows, sl], hb) + oh_ref[rows, sl]
            states[pr] = ht * dec_ref[ci, pr] + _dot_nt(hb, m_ref[ci, pr]) + nt_ref[ci, pr]
        yield
    for pr in range(RW_PAIRS):
        h_ref[pr] = states[pr]
    y = y_ref[...]
    mean = _head_sums(y, 1.0 / RW_HEAD)
    d = y - mean
    var = _head_sums(d * d, 1.0 / RW_HEAD)
    y = d * lax.rsqrt(var + RW_GN_EPS) * lnw_ref[...] + lnb_ref[...]
    o_ref[...] = ((y + bonus_ref[...]) * gate_ref[...]).astype(o_ref.dtype)


def _out_mix_body(blocks_per_seq, x_ref, ohg_ref, omla_ref, whg_ref, wmla_ref, wrw_ref,
                  m_ref, nt_ref, dec_ref, qh_ref, oh_ref, gate_ref, bonus_ref, lnw_ref, lnb_ref,
                  o_ref, orw_s, y_s, h_s):
    @pl.when(pl.program_id(0) % blocks_per_seq == 0)
    def _():
        h_s[...] = jnp.zeros_like(h_s)

    d = x_ref.shape[1]

    def projections():
        for c0 in range(0, d, OUT_COLS):
            cols = slice(c0, c0 + OUT_COLS)
            o_ref[:, cols] = (x_ref[:, cols] + _dot(ohg_ref[...], whg_ref[:, cols])
                              + _dot(omla_ref[...], wmla_ref[:, cols]))
            yield

    scan = _rw_scan_stages(m_ref, nt_ref, dec_ref, qh_ref, oh_ref, gate_ref, bonus_ref, lnw_ref, lnb_ref,
                           orw_s, y_s, h_s)
    _alongside(projections(), d // OUT_COLS, scan)
    o_ref[...] += _dot(orw_s[...], wrw_ref[...])


def out_mix(x, o_hg, o_mla, w_hg, w_mla, w_rw, m, nt, dec, qh, oh, gate, bonus, ln_w, ln_b, *, batch, bs):
    t, d = x.shape
    nc = bs // RW_CHUNK
    assert t % (batch * bs) == 0 and d // OUT_COLS >= nc
    row = lambda width: pl.BlockSpec((bs, width), lambda i: (i, 0))
    resident = lambda a: pl.BlockSpec(a.shape, lambda i: (0,) * a.ndim, pipeline_mode=pl.Buffered(1))
    mats = pl.BlockSpec((nc, RW_PAIRS, LANES, LANES), lambda i: (i, 0, 0, 0))
    decs = pl.BlockSpec((nc, RW_PAIRS, 1, LANES), lambda i: (i, 0, 0, 0))
    ln_w, ln_b = ln_w.reshape(1, -1), ln_b.reshape(1, -1)
    return pl.pallas_call(
        functools.partial(_out_mix_body, t // batch // bs),
        grid=(t // bs,),
        in_specs=[row(d), row(HG_WIDTH), row(MLA_WIDTH), resident(w_hg), resident(w_mla), resident(w_rw),
                  mats, mats, decs, row(RW_WIDTH), row(RW_WIDTH), row(RW_WIDTH), row(RW_WIDTH),
                  resident(ln_w), resident(ln_b)],
        out_specs=row(d),
        out_shape=jax.ShapeDtypeStruct((t, d), F32),
        scratch_shapes=[pltpu.VMEM((bs, RW_WIDTH), BF16), pltpu.VMEM((bs, RW_WIDTH), F32),
                        pltpu.VMEM((RW_PAIRS, LANES, LANES), F32)],
        compiler_params=_cparams(("arbitrary",), 56),
        name="out_mix",
    )(x, o_hg, o_mla, w_hg, w_mla, w_rw, m, nt, dec, qh, oh, gate, bonus, ln_w, ln_b)


def _rope_tables(positions):
    half = MLA_ROPE // 2
    inv_freq = ROPE_THETA ** (-jnp.arange(half, dtype=F32) / half)
    ang = positions.astype(F32).reshape(-1, 1) * inv_freq
    zeros = jnp.zeros((ang.shape[0], LANES - MLA_ROPE), F32)
    cos = jnp.concatenate([jnp.cos(ang), jnp.cos(ang), zeros], axis=1)
    sin = jnp.concatenate([jnp.sin(ang), jnp.sin(ang), zeros], axis=1)
    return cos, sin


def _pad_cols(a, width):
    return jnp.pad(a, [(0, 0)] * (a.ndim - 1) + [(0, width - a.shape[-1])])


def _forward(x, positions, attn_norm, w_in, hg_lower_bounds, hg_out_norm,
             mla_q_a_norm, mla_w_qb, mla_kv_a_norm, mla_w_kvb, mla_q_norm, mla_k_norm,
             rw_mu, rw_w0, rw_w2, rw_a0, rw_a2, rw_g2, rw_v0, rw_v1, rw_v2,
             rw_k_k, rw_k_a, rw_r_k, rw_ln_w, rw_ln_b,
             w_o, ffn_norm, w_gate_up, w_down, *, cfg):
    batch, seq, d = x.shape
    depth = w_in.shape[0]
    t = batch * seq
    x = x.reshape(t, d)

    lb_sm = jax.nn.softmax(hg_lower_bounds.astype(F32), axis=0)
    lower_bounds = jnp.cumsum(lb_sm, axis=0) - lb_sm[0]
    cos, sin = _rope_tables(positions)

    hg_end = 4 * HG_WIDTH
    mla_end = hg_end + MLA_IN
    w_in_hg = w_in[:, :, :hg_end].astype(BF16)
    w_in_mla = _pad_cols(w_in[:, :, hg_end:mla_end], MLA_IN_PAD).astype(BF16)
    w_in_rw = w_in[:, :, mla_end:].astype(BF16)
    w_qb = _pad_cols(mla_w_qb.reshape(depth, MLA_Q_RANK, MLA_HEADS, MLA_QK_DIM), MLA_QK_PAD)
    w_qb = w_qb.reshape(depth, MLA_Q_RANK, MLA_HEADS * MLA_QK_PAD).astype(BF16)
    w_kvb = mla_w_kvb.astype(BF16)
    split_gain = lambda g: jnp.stack([g[:, :MLA_NOPE], _pad_cols(g[:, MLA_NOPE:], LANES)], axis=1)
    g_q = split_gain(mla_q_norm)
    g_k = split_gain(mla_k_norm)
    w_o_hg = w_o[:, :HG_WIDTH].astype(BF16)
    w_o_mla = w_o[:, HG_WIDTH:HG_WIDTH + MLA_WIDTH].astype(BF16)
    w_o_rw = w_o[:, HG_WIDTH + MLA_WIDTH:].astype(BF16)
    w_gu = w_gate_up.astype(BF16)
    w_dn = w_down.astype(BF16)
    rw_w2b, rw_a2b, rw_g2b = rw_w2.astype(BF16), rw_a2.astype(BF16), rw_g2.astype(BF16)
    rw_v1b, rw_v2b = rw_v1.astype(BF16), rw_v2.astype(BF16)
    r_k = rw_r_k.reshape(depth, RW_WIDTH)

    v_first = None
    for l in range(depth):
        p_rw, q, k, v, o_hg = mixer_in(x, attn_norm[l], w_in_hg[l], w_in_mla[l], w_in_rw[l], cos, sin,
                                       mla_q_a_norm[l], w_qb[l], mla_kv_a_norm[l], w_kvb[l], g_q[l], g_k[l],
                                       lower_bounds[l], hg_out_norm[l], batch=batch, bm=cfg["proj_bm"])
        o_mla = flash_attn(q, k, v, batch=batch, blk=cfg["attn_blk"])

        vres = None if l == 0 else (rw_v0[l - 1], rw_v1b[l - 1], rw_v2b[l - 1], v_first)
        outs = rw_chunk(p_rw, rw_mu[l], rw_w0[l], rw_w2b[l], rw_a0[l], rw_a2b[l], rw_g2b[l],
                        rw_k_k[l], rw_k_a[l], r_k[l], vres, batch=batch, bt=cfg["rw_bt"])
        if l == 0:
            v_first = outs[7]
        x = out_mix(x, o_hg, o_mla, w_o_hg[l], w_o_mla[l], w_o_rw[l], *outs[:7], rw_ln_w[l], rw_ln_b[l],
                    batch=batch, bs=cfg["out_bm"])
        x = ffn(x, ffn_norm[l], w_gu[l], w_dn[l], bm=cfg["ffn_bm"], bf=cfg["ffn_bf"])
    return x.reshape(batch, seq, d)


_CFG = dict(proj_bm=256, attn_blk=1024, rw_bt=256,
            out_bm=512, ffn_bm=1024, ffn_bf=512)


def kernel(x, positions, attn_norm, w_in, hg_lower_bounds, hg_out_norm, mla_q_a_norm, mla_w_qb, mla_kv_a_norm,
           mla_w_kvb, mla_q_norm, mla_k_norm, rw_mu, rw_w0, rw_w2, rw_a0, rw_a2, rw_g2, rw_v0, rw_v1, rw_v2,
           rw_k_k, rw_k_a, rw_r_k, rw_ln_w, rw_ln_b, w_o, ffn_norm, w_gate_up, w_down):
    return _forward(x, positions, attn_norm, w_in, hg_lower_bounds, hg_out_norm, mla_q_a_norm, mla_w_qb,
                    mla_kv_a_norm, mla_w_kvb, mla_q_norm, mla_k_norm, rw_mu, rw_w0, rw_w2, rw_a0, rw_a2, rw_g2,
                    rw_v0, rw_v1, rw_v2, rw_k_k, rw_k_a, rw_r_k, rw_ln_w, rw_ln_b, w_o, ffn_norm, w_gate_up,
                    w_down, cfg=_CFG)
```

```python
import functools

import jax
import jax.numpy as jnp
from jax import lax
from jax.experimental import pallas as pl
from jax.experimental.pallas import tpu as pltpu

F32 = jnp.float32
BF16 = jnp.bfloat16

D_MODEL = 2048
DEPTH = 4
HG_WIDTH = 512
HG_HEAD_DIM = 128
HG_HEADS = HG_WIDTH // HG_HEAD_DIM
MLA_NOPE = 128
MLA_ROPE = 64
MLA_V = 128
MLA_HEADS = 8
MLA_WIDTH = MLA_HEADS * MLA_V
MLA_Q_RANK = 512
MLA_KV_RANK = 256
MLA_QK_DIM = MLA_NOPE + MLA_ROPE
MLA_QK_PAD = 256
MLA_IN = MLA_Q_RANK + MLA_KV_RANK + MLA_ROPE
MLA_IN_PAD = 896
ROPE_THETA = 10000.0
MASK_VALUE = -1e30
RW_WIDTH = 512
RW_HEAD = 64
RW_HEADS = RW_WIDTH // RW_HEAD
RW_PAIRS = RW_HEADS // 2
RW_W_RANK = 64
RW_A_RANK = 64
RW_V_RANK = 32
RW_G_RANK = 128
RW_IN = 3 * RW_WIDTH + RW_W_RANK + RW_A_RANK + RW_G_RANK
RW_GN_EPS = 64e-5
MIX_WIDTH = HG_WIDTH + MLA_WIDTH + RW_WIDTH
D_FF = 5632
RMS_EPS = 1e-6

HG_CHUNK = 32
HG_EXP_CLAMP = 80.0
HG_CHUNKS_PER_STEP = 4
RW_CHUNK = 64
RW_CHUNKS_PER_STEP = 4
ATTN_ROW_CHAINS = 4
PROJ_COLS = 512
OUT_COLS = 256
LANES = 128

_MIB = 1024 * 1024


def _cparams(semantics, vmem_mib):
    return pltpu.CompilerParams(dimension_semantics=semantics, vmem_limit_bytes=vmem_mib * _MIB)


def _dot(a, b):
    return jnp.dot(a, b, preferred_element_type=F32)


def _dot_nt(a, b):
    return lax.dot_general(a, b, (((1,), (1,)), ((), ())), preferred_element_type=F32)


def _dot_tn(a, b):
    return lax.dot_general(a, b, (((0,), (0,)), ((), ())), preferred_element_type=F32)


def _split(x, terms):
    parts = []
    for _ in range(terms - 1):
        hi = x.astype(BF16)
        parts.append(hi)
        x = x - hi.astype(F32)
    parts.append(x.astype(BF16))
    return parts


def _dot_sel_lhs(sel, x, terms=3):
    return sum(_dot(sel, part) for part in _split(x, terms))


def _dot_sel_rhs(x, sel, terms=3):
    return sum(_dot(part, sel) for part in _split(x, terms))


def _rms(x, width):
    ms = jnp.sum(x * x, axis=-1, keepdims=True) * (1.0 / width)
    return x * lax.rsqrt(ms + RMS_EPS)


def _ffn_body(x_ref, g_ref, wg_ref, wu_ref, wd_ref, o_ref, h_ref):
    @pl.when(pl.program_id(1) == 0)
    def _():
        x = x_ref[...]
        h_ref[...] = (_rms(x, x.shape[-1]) * g_ref[...]).astype(BF16)
        o_ref[...] = x

    h = h_ref[...]
    gate = _dot(h, wg_ref[...])
    up = _dot(h, wu_ref[...])
    act = (gate * jax.nn.sigmoid(gate) * up).astype(BF16)
    o_ref[...] += _dot(act, wd_ref[...])


def ffn(x, gain, w_gate_up, w_down, *, bm, bf):
    t, d = x.shape
    dff = w_down.shape[0]
    nf = dff // bf
    return pl.pallas_call(
        _ffn_body,
        grid=(t // bm, nf),
        in_specs=[
            pl.BlockSpec((bm, d), lambda i, j: (i, 0)),
            pl.BlockSpec((1, d), lambda i, j: (0, 0)),
            pl.BlockSpec((d, bf), lambda i, j: (0, j)),
            pl.BlockSpec((d, bf), lambda i, j: (0, j + nf)),
            pl.BlockSpec((bf, d), lambda i, j: (j, 0)),
        ],
        out_specs=pl.BlockSpec((bm, d), lambda i, j: (i, 0)),
        out_shape=jax.ShapeDtypeStruct((t, d), F32),
        scratch_shapes=[pltpu.VMEM((bm, d), BF16)],
        compiler_params=_cparams(("parallel", "arbitrary"), 56),
        name="ffn",
    )(x, gain.reshape(1, d), w_gate_up, w_gate_up, w_down)


def _hgrn2_stages(p_ref, lb_ref, gain_ref, o_ref, st_ref, first_chunk):
    c = HG_CHUNK
    w = HG_WIDTH
    row = lax.broadcasted_iota(jnp.int32, (c, c), 0)
    col = lax.broadcasted_iota(jnp.int32, (c, c), 1)
    tril = col <= row
    cum = tril.astype(BF16)
    lb = lb_ref[...]
    gain = gain_ref[...]
    heads = [slice(h * HG_HEAD_DIM, (h + 1) * HG_HEAD_DIM) for h in range(HG_HEADS)]
    states = [[st_ref[h] for h in range(HG_HEADS)]] + [None] * HG_CHUNKS_PER_STEP

    def chunk_stages(rows, u):
        q_raw = p_ref[rows, 0:w]
        z = p_ref[rows, w:2 * w]
        v = p_ref[rows, 2 * w:3 * w]
        g_raw = p_ref[rows, 3 * w:4 * w]
        q = q_raw * jax.nn.sigmoid(q_raw)
        log_f = jnp.log(lb + (1.0 - lb) * jax.nn.sigmoid(z))
        k = (1.0 - lb) * jax.nn.sigmoid(-z)
        b = _dot_sel_lhs(cum, log_f)
        yield
        b_mid = b[c // 2 - 1:c // 2, :]
        b_end = b[c - 1:c, :]
        q_in = (q * jnp.exp(jnp.minimum(b - b_mid, HG_EXP_CLAMP))).astype(BF16)
        k_in = (k * jnp.exp(jnp.minimum(b_mid - b, HG_EXP_CLAMP))).astype(BF16)
        q_st = (q * jnp.exp(b)).astype(BF16)
        k_st = (k * jnp.exp(b_end - b)).astype(BF16)
        d_end = jnp.exp(b_end)
        vb = v.astype(BF16)
        gate = g_raw * jax.nn.sigmoid(g_raw)
        scores = [jnp.where(tril, _dot_nt(q_in[:, sl], k_in[:, sl]), 0.0).astype(BF16) for sl in heads]
        updates = [_dot_tn(vb[:, sl], k_st[:, sl]) for sl in heads]
        yield
        st = states[u]
        states[u + 1] = [st[h] * d_end[:, sl] + updates[h] for h, sl in enumerate(heads)]
        outs = [_dot_nt(q_st[:, sl], st[h].astype(BF16)) + _dot(scores[h], vb[:, sl]) for h, sl in enumerate(heads)]
        yield
        for h, sl in enumerate(heads):
            o_ref[rows, sl] = (_rms(outs[h], HG_HEAD_DIM) * gain * gate[:, sl]).astype(o_ref.dtype)

    yield from _rounds([chunk_stages(pl.ds(pl.multiple_of((first_chunk + u) * c, c), c), u)
                        for u in range(HG_CHUNKS_PER_STEP)])
    for h in range(HG_HEADS):
        st_ref[h] = states[HG_CHUNKS_PER_STEP][h]


def _rope(x, cos, sin):
    half = MLA_ROPE // 2
    lane = lax.broadcasted_iota(jnp.int32, x.shape, 1)
    rot = jnp.where(lane < half, -pltpu.roll(x, LANES - half, 1), pltpu.roll(x, half, 1))
    return x * cos + rot * sin


def _mla_stages(p_ref, cos_ref, sin_ref, gqa_ref, wqb_ref, gkva_ref, wkvb_ref, gq_ref, gk_ref, q_ref, k_ref, v_ref):
    p = p_ref[...]
    cos = cos_ref[...]
    sin = sin_ref[...]
    scale = MLA_QK_DIM ** -0.5
    qn = (_rms(p[:, 0:MLA_Q_RANK], MLA_Q_RANK) * gqa_ref[...]).astype(BF16)
    q = _dot(qn, wqb_ref[...])
    kvn = (_rms(p[:, MLA_Q_RANK:MLA_Q_RANK + MLA_KV_RANK], MLA_KV_RANK) * gkva_ref[...]).astype(BF16)
    kv = _dot(kvn, wkvb_ref[...])
    gq_nope, gq_rope = gq_ref[0:1, :], gq_ref[1:2, :]
    gk_nope, gk_rope = gk_ref[0:1, :], gk_ref[1:2, :]
    k_rope = _rope(_rms(p[:, MLA_Q_RANK + MLA_KV_RANK:], MLA_ROPE) * gk_rope, cos, sin).astype(BF16)
    half = MLA_ROPE // 2
    first_half = lax.broadcasted_iota(jnp.int32, cos.shape, 1) < half
    yield

    def head_stages(h):
        o = h * MLA_QK_PAD
        q_nope, q_rope, k_nope = q[:, o:o + MLA_NOPE], q[:, o + MLA_NOPE:o + MLA_QK_PAD], kv[:, o:o + MLA_NOPE]
        ss_qn = jnp.sum(q_nope * q_nope, axis=-1, keepdims=True)
        ss_qr = jnp.sum(q_rope * q_rope, axis=-1, keepdims=True)
        ss_kn = jnp.sum(k_nope * k_nope, axis=-1, keepdims=True)
        yield
        q_nope = q_nope * lax.rsqrt(ss_qn * (1.0 / MLA_NOPE) + RMS_EPS) * (gq_nope * scale)
        q_rope = q_rope * lax.rsqrt(ss_qr * (1.0 / MLA_ROPE) + RMS_EPS) * (gq_rope * scale)
        k_nope = k_nope * lax.rsqrt(ss_kn * (1.0 / MLA_NOPE) + RMS_EPS) * gk_nope
        up, down = pltpu.roll(q_rope, LANES - half, 1), pltpu.roll(q_rope, half, 1)
        yield
        q_ref[h, :, 0:MLA_NOPE] = q_nope.astype(BF16)
        q_ref[h, :, MLA_NOPE:MLA_QK_PAD] = (q_rope * cos + jnp.where(first_half, -up, down) * sin).astype(BF16)
        k_ref[h, :, 0:MLA_NOPE] = k_nope.astype(BF16)
        k_ref[h, :, MLA_NOPE:MLA_QK_PAD] = k_rope
        v_ref[h] = kv[:, o + MLA_NOPE:o + MLA_QK_PAD].astype(BF16)

    yield from _rounds([head_stages(h) for h in range(MLA_HEADS)])


def _mixer_in_body(blocks_per_seq, x_ref, g_ref, whg_ref, wmla_ref, wrw_ref,
                   cos_ref, sin_ref, gqa_ref, wqb_ref, gkva_ref, wkvb_ref, gq_ref, gk_ref, lb_ref, hgain_ref,
                   prw_ref, q_ref, k_ref, v_ref, ohg_ref,
                   h_s, phg_s, pmla_s, st_ref):
    @pl.when(pl.program_id(0) % blocks_per_seq == 0)
    def _():
        st_ref[...] = jnp.zeros_like(st_ref)

    h_s[...] = (_rms(x_ref[...], x_ref.shape[-1]) * g_ref[...]).astype(BF16)

    def projections():
        pmla_s[...] = _dot(h_s[...], wmla_ref[...])
        yield
        for c0 in range(0, whg_ref.shape[1], PROJ_COLS):
            phg_s[:, c0:c0 + PROJ_COLS] = _dot(h_s[...], whg_ref[:, c0:c0 + PROJ_COLS])
            yield
        for c0 in range(0, wrw_ref.shape[1], PROJ_COLS):
            c1 = min(c0 + PROJ_COLS, wrw_ref.shape[1])
            prw_ref[:, c0:c1] = _dot(h_s[...], wrw_ref[:, c0:c1])
            yield

    def hgrn2_blocks():
        for first in range(0, x_ref.shape[0] // HG_CHUNK, HG_CHUNKS_PER_STEP):
            yield from _hgrn2_stages(phg_s, lb_ref, hgain_ref, ohg_ref, st_ref, first)

    proj = projections()
    next(proj)
    _alongside(proj, whg_ref.shape[1] // PROJ_COLS,
               _mla_stages(pmla_s, cos_ref, sin_ref, gqa_ref, wqb_ref, gkva_ref, wkvb_ref, gq_ref, gk_ref,
                           q_ref, k_ref, v_ref))
    _alongside(proj, pl.cdiv(wrw_ref.shape[1], PROJ_COLS), hgrn2_blocks())


def mixer_in(x, gain, w_hg, w_mla, w_rw, cos, sin, g_qa, w_qb, g_kva, w_kvb, g_q, g_k, lower_bound, out_gain,
             *, batch, bm):
    t, d = x.shape
    assert t % (batch * bm) == 0 and bm % (HG_CHUNK * HG_CHUNKS_PER_STEP) == 0
    resident = lambda a: pl.BlockSpec(a.shape, lambda i: (0,) * a.ndim, pipeline_mode=pl.Buffered(1))
    rows = lambda width: pl.BlockSpec((bm, width), lambda i: (i, 0))
    heads = lambda width: pl.BlockSpec((MLA_HEADS, bm, width), lambda i: (0, i, 0))
    params = [gain.reshape(1, d), w_hg, w_mla, w_rw]
    mla_params = [g_qa.reshape(1, -1), w_qb, g_kva.reshape(1, -1), w_kvb, g_q, g_k]
    hg_params = [lower_bound.reshape(1, HG_WIDTH), out_gain.reshape(1, HG_HEAD_DIM)]
    return pl.pallas_call(
        functools.partial(_mixer_in_body, t // batch // bm),
        grid=(t // bm,),
        in_specs=([rows(d)] + [resident(a) for a in params] + [rows(LANES), rows(LANES)]
                  + [resident(a) for a in mla_params + hg_params]),
        out_specs=[rows(w_rw.shape[1]), heads(MLA_QK_PAD), heads(MLA_QK_PAD), heads(MLA_V), rows(HG_WIDTH)],
        out_shape=[jax.ShapeDtypeStruct((t, w_rw.shape[1]), F32),
                   jax.ShapeDtypeStruct((MLA_HEADS, t, MLA_QK_PAD), BF16),
                   jax.ShapeDtypeStruct((MLA_HEADS, t, MLA_QK_PAD), BF16),
                   jax.ShapeDtypeStruct((MLA_HEADS, t, MLA_V), BF16),
                   jax.ShapeDtypeStruct((t, HG_WIDTH), BF16)],
        scratch_shapes=[pltpu.VMEM((bm, d), BF16), pltpu.VMEM((bm, w_hg.shape[1]), F32),
                        pltpu.VMEM((bm, w_mla.shape[1]), F32),
                        pltpu.VMEM((HG_HEADS, HG_HEAD_DIM, HG_HEAD_DIM), F32)],
        compiler_params=_cparams(("arbitrary",), 56),
        name="mixer_in",
    )(x, *params, cos, sin, *mla_params, *hg_params)


def _pipelined(stages):
    n = len(stages)
    done = [False] * n
    rnd = 0
    while not all(done):
        for i in range(min(rnd, n - 1) + 1):
            if not done[i]:
                try:
                    next(stages[i])
                except StopIteration:
                    done[i] = True
        rnd += 1


def _flash_body(q_ref, k_ref, v_ref, o_ref):
    i = pl.program_id(2)
    blk = q_ref.shape[1]
    rb = blk // ATTN_ROW_CHAINS

    def unit(state, r, j, diagonal):
        rows = slice(r * rb, (r + 1) * rb)
        ncol = (r + 1) * rb if diagonal else blk
        keys = slice(j * blk, j * blk + ncol)
        s = _dot_nt(q_ref[0, rows, :], k_ref[0, keys, :])
        yield
        if diagonal:
            row = lax.broadcasted_iota(jnp.int32, (rb, rb), 0)
            col = lax.broadcasted_iota(jnp.int32, (rb, rb), 1)
            last = jnp.where(col <= row, s[:, r * rb:], MASK_VALUE)
            s = jnp.concatenate([s[:, :r * rb], last], axis=1) if r else last
        m_old, l_old, acc_old = state[r]
        m_new = jnp.maximum(m_old, jnp.max(s, axis=-1, keepdims=True))
        alpha = jnp.exp(m_old - m_new)
        p = jnp.exp(s - jnp.concatenate([m_new] * (ncol // LANES), axis=1))
        l_new = alpha * l_old + jnp.sum(p, axis=-1, keepdims=True)
        pb = p.astype(BF16)
        yield
        acc = alpha * acc_old + _dot(pb, v_ref[0, keys, :])
        if diagonal:
            o_ref[rows, :] = (acc / l_new).astype(o_ref.dtype)
        else:
            state[r] = (m_new, l_new, acc)

    def query_block(nfull):
        start = (jnp.full((rb, LANES), MASK_VALUE, F32), jnp.zeros((rb, LANES), F32), jnp.zeros((rb, MLA_V), F32))
        state = [start] * ATTN_ROW_CHAINS
        units = [unit(state, r, j, False) for j in range(nfull) for r in range(ATTN_ROW_CHAINS)]
        units += [unit(state, r, nfull, True) for r in range(ATTN_ROW_CHAINS)]
        _pipelined(units)

    for nfull in range(k_ref.shape[1] // blk):
        pl.when(i == nfull)(functools.partial(query_block, nfull))


def flash_attn(q, k, v, *, batch, blk):
    heads, t, _ = q.shape
    seq = t // batch
    nb = seq // blk
    assert t == batch * nb * blk and blk % (ATTN_ROW_CHAINS * LANES) == 0
    kv = lambda width: pl.BlockSpec((1, seq, width), lambda b, h, i: (h, b, 0))
    return pl.pallas_call(
        _flash_body,
        grid=(batch, heads, nb),
        in_specs=[pl.BlockSpec((1, blk, MLA_QK_PAD), lambda b, h, i: (h, b * nb + i, 0)), kv(MLA_QK_PAD), kv(MLA_V)],
        out_specs=pl.BlockSpec((blk, MLA_V), lambda b, h, i: (b * nb + i, h)),
        out_shape=jax.ShapeDtypeStruct((t, heads * MLA_V), BF16),
        compiler_params=_cparams(("parallel", "parallel", "arbitrary"), 48),
        name="flash_attn",
    )(q, k, v)


def _head_sums(x, weight):
    half = 2 * LANES
    gi = lax.broadcasted_iota(jnp.int32, (half, half), 0) // RW_HEAD
    gj = lax.broadcasted_iota(jnp.int32, (half, half), 1) // RW_HEAD
    sel = jnp.where(gi == gj, weight, 0.0).astype(BF16)
    return jnp.concatenate([_dot_sel_rhs(x[:, 0:half], sel, 2), _dot_sel_rhs(x[:, half:], sel, 2)], axis=1)


def _rounds(stages):
    results = [None] * len(stages)
    live = list(range(len(stages)))
    while live:
        for i in list(live):
            try:
                next(stages[i])
            except StopIteration as stop:
                results[i] = stop.value
                live.remove(i)
        yield
    return results


def _lock_step(stages):
    rounds = _rounds(stages)
    while True:
        try:
            next(rounds)
        except StopIteration as stop:
            return stop.value


def _alongside(main, steps, side):
    for _ in range(steps):
        next(main)
        next(side, None)
    for _ in side:
        pass


def _unit_lower_inverse(a, block):
    n = a.shape[0]
    row = lax.broadcasted_iota(jnp.int32, (n, n), 0)
    col = lax.broadcasted_iota(jnp.int32, (n, n), 1)
    diff = row ^ col
    t = jnp.where(row == col, 1.0, 0.0) + jnp.where(diff < 2, a, 0.0)
    size = 2
    while size < block:
        off = jnp.where((diff >= size) & (diff < 2 * size), a, 0.0).astype(BF16)
        tb = t.astype(BF16)
        half = _dot(tb, off).astype(BF16)
        yield
        t = t + _dot(half, tb)
        yield
        size *= 2
    return t


def _rw_chunk_body(has_vres, *refs):
    if has_vres:
        (p_ref, prev_ref, mu_ref, w0_ref, w2_ref, a0_ref, a2_ref, g2_ref, kk_ref, ka_ref, rk_ref,
         v0_ref, v1_ref, v2_ref, vf_ref,
         m_ref, nt_ref, dec_ref, qh_ref, oh_ref, gate_ref, bonus_ref,
         r_s, g_s, k_s, v_s, kk_s, b_s) = refs
    else:
        (p_ref, prev_ref, mu_ref, w0_ref, w2_ref, a0_ref, a2_ref, g2_ref, kk_ref, ka_ref, rk_ref,
         m_ref, nt_ref, dec_ref, qh_ref, oh_ref, gate_ref, bonus_ref, vout_ref,
         r_s, g_s, k_s, v_s, kk_s, b_s) = refs
    c = RW_CHUNK
    w = RW_WIDTH
    bt = p_ref.shape[0]

    p = p_ref[...]
    first = pl.program_id(1) == 0
    prev_row = jnp.where(first, 0.0, prev_ref[7:8, :])
    rowi = lax.broadcasted_iota(jnp.int32, p.shape, 0)
    prev = jnp.where(rowi == 0, prev_row, pltpu.roll(p, 1, 0))
    p = p + (prev - p) * mu_ref[...]
    r = p[:, 0:w]
    k = p[:, w:2 * w]
    v = p[:, 2 * w:3 * w]
    wd = p[:, 3 * w:3 * w + RW_W_RANK]
    ad = p[:, 3 * w + RW_W_RANK:3 * w + RW_W_RANK + RW_A_RANK]
    gd = p[:, 3 * w + RW_W_RANK + RW_A_RANK:]
    w_log = -jax.nn.softplus(-(w0_ref[...] + _dot(jnp.tanh(wd).astype(BF16), w2_ref[...]))) - 0.5
    lr = jax.nn.sigmoid(a0_ref[...] + _dot(ad.astype(BF16), a2_ref[...]))
    gate = _dot(jax.nn.sigmoid(gd).astype(BF16), g2_ref[...])
    if has_vres:
        mix = jax.nn.sigmoid(v0_ref[...] + _dot(_dot(v.astype(BF16), v1_ref[...]).astype(BF16), v2_ref[...]))
        v = v + (vf_ref[...] - v) * mix
    else:
        vout_ref[...] = v
    kk = k * kk_ref[...]
    kk = kk / jnp.maximum(jnp.sqrt(_head_sums(kk * kk, 1.0)), 1e-12)
    k = k * (1.0 + (lr - 1.0) * ka_ref[...])
    gate_ref[...] = gate
    bonus_ref[...] = _head_sums(r * k * rk_ref[...], 1.0) * v
    r_s[...] = r
    g_s[...] = -jnp.exp(w_log)
    k_s[...] = k
    v_s[...] = v
    kk_s[...] = kk
    b_s[...] = kk * lr

    n2 = 2 * c
    row = lax.broadcasted_iota(jnp.int32, (n2, n2), 0)
    col = lax.broadcasted_iota(jnp.int32, (n2, n2), 1)
    same_head = (row ^ col) < c
    incl = same_head & (col <= row)
    strict = same_head & (col < row)
    ci_row = lax.broadcasted_iota(jnp.int32, (c, c), 0)
    ci_col = lax.broadcasted_iota(jnp.int32, (c, c), 1)
    cum = (ci_col <= ci_row).astype(BF16)
    lane = lax.broadcasted_iota(jnp.int32, (1, LANES), 1)
    head0 = lane < RW_HEAD

    def stack(z):
        return jnp.concatenate([jnp.where(head0, z, 0.0), jnp.where(head0, 0.0, z)], axis=0)

    def pair_matrices(g, r, k, v, kkc, bc):
        gc = _dot_sel_lhs(cum, g)
        yield
        g_end = gc[c - 1:c, :]
        inv = jnp.exp(-gc)
        to_end = jnp.exp(g_end - gc)
        a2 = stack(-kkc * jnp.exp(gc - g)).astype(BF16)
        r2 = stack(r * jnp.exp(gc))
        v2 = stack(v).astype(BF16)
        b_end = stack(bc * to_end).astype(BF16)
        k_end = stack(k * to_end).astype(BF16)
        b_inv = (bc * inv).astype(BF16)
        k_inv = (k * inv).astype(BF16)
        ar = jnp.concatenate([a2, r2.astype(BF16)], axis=0)
        abk = _dot_nt(ar, jnp.concatenate([b_inv, b_inv, k_inv, k_inv], axis=0))
        yield
        a_ab = jnp.where(strict, abk[0:n2, 0:n2], 0.0)
        a_rb = jnp.where(incl, abk[n2:2 * n2, 0:n2], 0.0).astype(BF16)
        a_ak = jnp.where(strict, abk[0:n2, n2:2 * n2], 0.0).astype(BF16)
        a_rk = jnp.where(incl, abk[n2:2 * n2, n2:2 * n2], 0.0).astype(BF16)
        x = _dot(a_ak, v2).astype(BF16)
        nt = _dot_tn(v2, k_end)
        oh2 = _dot(a_rk, v2)
        t_inv = (yield from _unit_lower_inverse(a_ab, c)).astype(BF16)
        p12 = _dot(t_inv, jnp.concatenate([a2, x], axis=1)).astype(BF16)
        yield
        qo = _dot(a_rb, p12)
        qh2 = r2 + qo[:, 0:LANES]
        oh2 = oh2 + qo[:, LANES:2 * LANES]
        m = _dot_tn(b_end, p12[:, 0:LANES])
        nt = nt + _dot_tn(p12[:, LANES:2 * LANES], b_end)
        return m, nt, jnp.exp(g_end), qh2[0:c] + qh2[c:n2], oh2[0:c] + oh2[c:n2]

    def chunk(ci, carry):
        units = [(ci * RW_CHUNKS_PER_STEP + u, pr, slice(pr * LANES, (pr + 1) * LANES))
                 for u in range(RW_CHUNKS_PER_STEP) for pr in range(RW_PAIRS)]
        rows = lambda cj: pl.ds(pl.multiple_of(cj * c, c), c)
        loaded = [[s[rows(cj), sl] for s in (g_s, r_s, k_s, v_s, kk_s, b_s)] for cj, _, sl in units]
        results = _lock_step([pair_matrices(*operands) for operands in loaded])
        for (cj, pr, sl), (m, nt, dec, qh, oh) in zip(units, results):
            m_ref[cj, pr] = m.astype(m_ref.dtype)
            nt_ref[cj, pr] = nt
            dec_ref[cj, pr] = dec
            qh_ref[rows(cj), sl] = qh.astype(qh_ref.dtype)
            oh_ref[rows(cj), sl] = oh
        return carry

    lax.fori_loop(0, bt // c // RW_CHUNKS_PER_STEP, chunk, 0)


def rw_chunk(p_rw, mu, w0, w2, a0, a2, g2, k_k, k_a, r_k, vres, *, batch, bt):
    t = p_rw.shape[0]
    nb = t // batch // bt
    nc = bt // RW_CHUNK
    assert t == batch * nb * bt and bt % (RW_CHUNK * RW_CHUNKS_PER_STEP) == 0
    has_vres = vres is not None
    row1 = lambda a: a.reshape(1, -1)
    full = lambda a: pl.BlockSpec(a.shape, lambda b, s: (0,) * a.ndim)
    rows = lambda width: pl.BlockSpec((bt, width), lambda b, s: (b * nb + s, 0))
    prev_spec = pl.BlockSpec((8, RW_IN), lambda b, s: (jnp.maximum((b * nb + s) * (bt // 8) - 1, 0), 0))
    mats = pl.BlockSpec((nc, RW_PAIRS, LANES, LANES), lambda b, s: (b * nb + s, 0, 0, 0))
    params = [row1(mu), row1(w0), w2, row1(a0), a2, g2, row1(k_k), row1(k_a), row1(r_k)]
    args = [p_rw, p_rw] + params
    in_specs = [rows(RW_IN), prev_spec] + [full(a) for a in params]
    if has_vres:
        v0, v1, v2, v_first = vres
        extra = [row1(v0), v1, v2]
        args += extra + [v_first]
        in_specs += [full(a) for a in extra] + [rows(RW_WIDTH)]
    decs = pl.BlockSpec((nc, RW_PAIRS, 1, LANES), lambda b, s: (b * nb + s, 0, 0, 0))
    mat_shape = lambda dt: jax.ShapeDtypeStruct((t // RW_CHUNK, RW_PAIRS, LANES, LANES), dt)
    dec_shape = jax.ShapeDtypeStruct((t // RW_CHUNK, RW_PAIRS, 1, LANES), F32)
    tok_shape = lambda dt: jax.ShapeDtypeStruct((t, RW_WIDTH), dt)
    out_specs = [mats, mats, decs, rows(RW_WIDTH), rows(RW_WIDTH), rows(RW_WIDTH), rows(RW_WIDTH)]
    out_shape = [mat_shape(BF16), mat_shape(F32), dec_shape, tok_shape(BF16), tok_shape(F32), tok_shape(F32),
                 tok_shape(F32)]
    if not has_vres:
        out_specs.append(rows(RW_WIDTH))
        out_shape.append(tok_shape(F32))
    return pl.pallas_call(
        functools.partial(_rw_chunk_body, has_vres),
        grid=(batch, nb),
        in_specs=in_specs,
        out_specs=out_specs,
        out_shape=out_shape,
        scratch_shapes=[pltpu.VMEM((bt, RW_WIDTH), F32) for _ in range(6)],
        compiler_params=_cparams(("parallel", "arbitrary"), 48),
        name="rw_chunk",
    )(*args)


def _rw_scan_stages(m_ref, nt_ref, dec_ref, qh_ref, oh_ref, gate_ref, bonus_ref, lnw_ref, lnb_ref, o_ref, y_ref, h_ref):
    c = RW_CHUNK
    states = [h_ref[pr] for pr in range(RW_PAIRS)]
    for ci in range(qh_ref.shape[0] // c):
        rows = slice(ci * c, (ci + 1) * c)
        for pr in range(RW_PAIRS):
            sl = slice(pr * LANES, (pr + 1) * LANES)
            ht = states[pr]
            hb = ht.astype(BF16)
            y_ref[rows, sl] = _dot_nt(qh_ref[rows, sl], hb) + oh_ref[rows, sl]
            states[pr] = ht * dec_ref[ci, pr] + _dot_nt(hb, m_ref[ci, pr]) + nt_ref[ci, pr]
        yield
    for pr in range(RW_PAIRS):
        h_ref[pr] = states[pr]
    y = y_ref[...]
    mean = _head_sums(y, 1.0 / RW_HEAD)
    d = y - mean
    var = _head_sums(d * d, 1.0 / RW_HEAD)
    y = d * lax.rsqrt(var + RW_GN_EPS) * lnw_ref[...] + lnb_ref[...]
    o_ref[...] = ((y + bonus_ref[...]) * gate_ref[...]).astype(o_ref.dtype)


def _out_mix_body(blocks_per_seq, x_ref, ohg_ref, omla_ref, whg_ref, wmla_ref, wrw_ref,
                  m_ref, nt_ref, dec_ref, qh_ref, oh_ref, gate_ref, bonus_ref, lnw_ref, lnb_ref,
                  o_ref, orw_s, y_s, h_s):
    @pl.when(pl.program_id(0) % blocks_per_seq == 0)
    def _():
        h_s[...] = jnp.zeros_like(h_s)

    d = x_ref.shape[1]

    def projections():
        for c0 in range(0, d, OUT_COLS):
            cols = slice(c0, c0 + OUT_COLS)
            o_ref[:, cols] = (x_ref[:, cols] + _dot(ohg_ref[...], whg_ref[:, cols])
                              + _dot(omla_ref[...], wmla_ref[:, cols]))
            yield

    scan = _rw_scan_stages(m_ref, nt_ref, dec_ref, qh_ref, oh_ref, gate_ref, bonus_ref, lnw_ref, lnb_ref,
                           orw_s, y_s, h_s)
    _alongside(projections(), d // OUT_COLS, scan)
    o_ref[...] += _dot(orw_s[...], wrw_ref[...])


def out_mix(x, o_hg, o_mla, w_hg, w_mla, w_rw, m, nt, dec, qh, oh, gate, bonus, ln_w, ln_b, *, batch, bs):
    t, d = x.shape
    nc = bs // RW_CHUNK
    assert t % (batch * bs) == 0 and d // OUT_COLS >= nc
    row = lambda width: pl.BlockSpec((bs, width), lambda i: (i, 0))
    resident = lambda a: pl.BlockSpec(a.shape, lambda i: (0,) * a.ndim, pipeline_mode=pl.Buffered(1))
    mats = pl.BlockSpec((nc, RW_PAIRS, LANES, LANES), lambda i: (i, 0, 0, 0))
    decs = pl.BlockSpec((nc, RW_PAIRS, 1, LANES), lambda i: (i, 0, 0, 0))
    ln_w, ln_b = ln_w.reshape(1, -1), ln_b.reshape(1, -1)
    return pl.pallas_call(
        functools.partial(_out_mix_body, t // batch // bs),
        grid=(t // bs,),
        in_specs=[row(d), row(HG_WIDTH), row(MLA_WIDTH), resident(w_hg), resident(w_mla), resident(w_rw),
                  mats, mats, decs, row(RW_WIDTH), row(RW_WIDTH), row(RW_WIDTH), row(RW_WIDTH),
                  resident(ln_w), resident(ln_b)],
        out_specs=row(d),
        out_shape=jax.ShapeDtypeStruct((t, d), F32),
        scratch_shapes=[pltpu.VMEM((bs, RW_WIDTH), BF16), pltpu.VMEM((bs, RW_WIDTH), F32),
                        pltpu.VMEM((RW_PAIRS, LANES, LANES), F32)],
        compiler_params=_cparams(("arbitrary",), 56),
        name="out_mix",
    )(x, o_hg, o_mla, w_hg, w_mla, w_rw, m, nt, dec, qh, oh, gate, bonus, ln_w, ln_b)


def _rope_tables(positions):
    half = MLA_ROPE // 2
    inv_freq = ROPE_THETA ** (-jnp.arange(half, dtype=F32) / half)
    ang = positions.astype(F32).reshape(-1, 1) * inv_freq
    zeros = jnp.zeros((ang.shape[0], LANES - MLA_ROPE), F32)
    cos = jnp.concatenate([jnp.cos(ang), jnp.cos(ang), zeros], axis=1)
    sin = jnp.concatenate([jnp.sin(ang), jnp.sin(ang), zeros], axis=1)
    return cos, sin


def _pad_cols(a, width):
    return jnp.pad(a, [(0, 0)] * (a.ndim - 1) + [(0, width - a.shape[-1])])


def _forward(x, positions, attn_norm, w_in, hg_lower_bounds, hg_out_norm,
             mla_q_a_norm, mla_w_qb, mla_kv_a_norm, mla_w_kvb, mla_q_norm, mla_k_norm,
             rw_mu, rw_w0, rw_w2, rw_a0, rw_a2, rw_g2, rw_v0, rw_v1, rw_v2,
             rw_k_k, rw_k_a, rw_r_k, rw_ln_w, rw_ln_b,
             w_o, ffn_norm, w_gate_up, w_down, *, cfg):
    batch, seq, d = x.shape
    depth = w_in.shape[0]
    t = batch * seq
    x = x.reshape(t, d)

    lb_sm = jax.nn.softmax(hg_lower_bounds.astype(F32), axis=0)
    lower_bounds = jnp.cumsum(lb_sm, axis=0) - lb_sm[0]
    cos, sin = _rope_tables(positions)

    hg_end = 4 * HG_WIDTH
    mla_end = hg_end + MLA_IN
    w_in_hg = w_in[:, :, :hg_end].astype(BF16)
    w_in_mla = _pad_cols(w_in[:, :, hg_end:mla_end], MLA_IN_PAD).astype(BF16)
    w_in_rw = w_in[:, :, mla_end:].astype(BF16)
    w_qb = _pad_cols(mla_w_qb.reshape(depth, MLA_Q_RANK, MLA_HEADS, MLA_QK_DIM), MLA_QK_PAD)
    w_qb = w_qb.reshape(depth, MLA_Q_RANK, MLA_HEADS * MLA_QK_PAD).astype(BF16)
    w_kvb = mla_w_kvb.astype(BF16)
    split_gain = lambda g: jnp.stack([g[:, :MLA_NOPE], _pad_cols(g[:, MLA_NOPE:], LANES)], axis=1)
    g_q = split_gain(mla_q_norm)
    g_k = split_gain(mla_k_norm)
    w_o_hg = w_o[:, :HG_WIDTH].astype(BF16)
    w_o_mla = w_o[:, HG_WIDTH:HG_WIDTH + MLA_WIDTH].astype(BF16)
    w_o_rw = w_o[:, HG_WIDTH + MLA_WIDTH:].astype(BF16)
    w_gu = w_gate_up.astype(BF16)
    w_dn = w_down.astype(BF16)
    rw_w2b, rw_a2b, rw_g2b = rw_w2.astype(BF16), rw_a2.astype(BF16), rw_g2.astype(BF16)
    rw_v1b, rw_v2b = rw_v1.astype(BF16), rw_v2.astype(BF16)
    r_k = rw_r_k.reshape(depth, RW_WIDTH)

    v_first = None
    for l in range(depth):
        p_rw, q, k, v, o_hg = mixer_in(x, attn_norm[l], w_in_hg[l], w_in_mla[l], w_in_rw[l], cos, sin,
                                       mla_q_a_norm[l], w_qb[l], mla_kv_a_norm[l], w_kvb[l], g_q[l], g_k[l],
                                       lower_bounds[l], hg_out_norm[l], batch=batch, bm=cfg["proj_bm"])
        o_mla = flash_attn(q, k, v, batch=batch, blk=cfg["attn_blk"])

        vres = None if l == 0 else (rw_v0[l - 1], rw_v1b[l - 1], rw_v2b[l - 1], v_first)
        outs = rw_chunk(p_rw, rw_mu[l], rw_w0[l], rw_w2b[l], rw_a0[l], rw_a2b[l], rw_g2b[l],
                        rw_k_k[l], rw_k_a[l], r_k[l], vres, batch=batch, bt=cfg["rw_bt"])
        if l == 0:
            v_first = outs[7]
        x = out_mix(x, o_hg, o_mla, w_o_hg[l], w_o_mla[l], w_o_rw[l], *outs[:7], rw_ln_w[l], rw_ln_b[l],
                    batch=batch, bs=cfg["out_bm"])
        x = ffn(x, ffn_norm[l], w_gu[l], w_dn[l], bm=cfg["ffn_bm"], bf=cfg["ffn_bf"])
    return x.reshape(batch, seq, d)


_CFG = dict(proj_bm=256, attn_blk=1024, rw_bt=256,
            out_bm=512, ffn_bm=1024, ffn_bf=512)


def kernel(x, positions, attn_norm, w_in, hg_lower_bounds, hg_out_norm, mla_q_a_norm, mla_w_qb, mla_kv_a_norm,
           mla_w_kvb, mla_q_norm, mla_k_norm, rw_mu, rw_w0, rw_w2, rw_a0, rw_a2, rw_g2, rw_v0, rw_v1, rw_v2,
           rw_k_k, rw_k_a, rw_r_k, rw_ln_w, rw_ln_b, w_o, ffn_norm, w_gate_up, w_down):
    return _forward(x, positions, attn_norm, w_in, hg_lower_bounds, hg_out_norm, mla_q_a_norm, mla_w_qb,
                    mla_kv_a_norm, mla_w_kvb, mla_q_norm, mla_k_norm, rw_mu, rw_w0, rw_w2, rw_a0, rw_a2, rw_g2,
                    rw_v0, rw_v1, rw_v2, rw_k_k, rw_k_a, rw_r_k, rw_ln_w, rw_ln_b, w_o, ffn_norm, w_gate_up,
                    w_down, cfg=_CFG)
```

```python
import functools

import jax
import jax.numpy as jnp
from jax import lax
from jax.experimental import pallas as pl
from jax.experimental.pallas import tpu as pltpu

F32 = jnp.float32
BF16 = jnp.bfloat16

D_MODEL = 2048
DEPTH = 4
HG_WIDTH = 512
HG_HEAD_DIM = 128
HG_HEADS = HG_WIDTH // HG_HEAD_DIM
MLA_NOPE = 128
MLA_ROPE = 64
MLA_V = 128
MLA_HEADS = 8
MLA_WIDTH = MLA_HEADS * MLA_V
MLA_Q_RANK = 512
MLA_KV_RANK = 256
MLA_QK_DIM = MLA_NOPE + MLA_ROPE
MLA_QK_PAD = 256
MLA_IN = MLA_Q_RANK + MLA_KV_RANK + MLA_ROPE
MLA_IN_PAD = 896
ROPE_THETA = 10000.0
MASK_VALUE = -1e30
RW_WIDTH = 512
RW_HEAD = 64
RW_HEADS = RW_WIDTH // RW_HEAD
RW_PAIRS = RW_HEADS // 2
RW_W_RANK = 64
RW_A_RANK = 64
RW_V_RANK = 32
RW_G_RANK = 128
RW_IN = 3 * RW_WIDTH + RW_W_RANK + RW_A_RANK + RW_G_RANK
RW_GN_EPS = 64e-5
MIX_WIDTH = HG_WIDTH + MLA_WIDTH + RW_WIDTH
D_FF = 5632
RMS_EPS = 1e-6

HG_CHUNK = 32
HG_SUB = 8
HG_CHUNKS_PER_STEP = 4
RW_CHUNK = 64
RW_CHUNKS_PER_STEP = 4
ATTN_ROW_CHAINS = 4
PROJ_COLS = 512
OUT_COLS = 256
LANES = 128

_MIB = 1024 * 1024


def _cparams(semantics, vmem_mib):
    return pltpu.CompilerParams(dimension_semantics=semantics, vmem_limit_bytes=vmem_mib * _MIB)


def _dot(a, b):
    return jnp.dot(a, b, preferred_element_type=F32)


def _dot_nt(a, b):
    return lax.dot_general(a, b, (((1,), (1,)), ((), ())), preferred_element_type=F32)


def _dot_tn(a, b):
    return lax.dot_general(a, b, (((0,), (0,)), ((), ())), preferred_element_type=F32)


def _split(x, terms):
    parts = []
    for _ in range(terms - 1):
        hi = x.astype(BF16)
        parts.append(hi)
        x = x - hi.astype(F32)
    parts.append(x.astype(BF16))
    return parts


def _dot_sel_lhs(sel, x, terms=3):
    return sum(_dot(sel, part) for part in _split(x, terms))


def _dot_sel_rhs(x, sel, terms=3):
    return sum(_dot(part, sel) for part in _split(x, terms))


def _rms(x, width):
    ms = jnp.sum(x * x, axis=-1, keepdims=True) * (1.0 / width)
    return x * lax.rsqrt(ms + RMS_EPS)


def _ffn_body(x_ref, g_ref, wg_ref, wu_ref, wd_ref, o_ref, h_ref):
    @pl.when(pl.program_id(1) == 0)
    def _():
        x = x_ref[...]
        h_ref[...] = (_rms(x, x.shape[-1]) * g_ref[...]).astype(BF16)
        o_ref[...] = x

    h = h_ref[...]
    gate = _dot(h, wg_ref[...])
    up = _dot(h, wu_ref[...])
    act = (gate * jax.nn.sigmoid(gate) * up).astype(BF16)
    o_ref[...] += _dot(act, wd_ref[...])


def ffn(x, gain, w_gate_up, w_down, *, bm, bf):
    t, d = x.shape
    dff = w_down.shape[0]
    nf = dff // bf
    return pl.pallas_call(
        _ffn_body,
        grid=(t // bm, nf),
        in_specs=[
            pl.BlockSpec((bm, d), lambda i, j: (i, 0)),
            pl.BlockSpec((1, d), lambda i, j: (0, 0)),
            pl.BlockSpec((d, bf), lambda i, j: (0, j)),
            pl.BlockSpec((d, bf), lambda i, j: (0, j + nf)),
            pl.BlockSpec((bf, d), lambda i, j: (j, 0)),
        ],
        out_specs=pl.BlockSpec((bm, d), lambda i, j: (i, 0)),
        out_shape=jax.ShapeDtypeStruct((t, d), F32),
        scratch_shapes=[pltpu.VMEM((bm, d), BF16)],
        compiler_params=_cparams(("parallel", "arbitrary"), 56),
        name="ffn",
    )(x, gain.reshape(1, d), w_gate_up, w_gate_up, w_down)


def _hgrn2_scores(q, k, b):
    c, sub = HG_CHUNK, HG_SUB
    nsub = c // sub
    row = lax.broadcasted_iota(jnp.int32, (c, c), 0)
    col = lax.broadcasted_iota(jnp.int32, (c, c), 1)
    blk_t, blk_s = row // sub, col // sub
    ends = [b[sub * i + sub - 1:sub * i + sub, :] for i in range(nsub)]
    starts = [jnp.zeros_like(ends[0])] + ends[:-1]
    per_block = lambda vals: jnp.concatenate([jnp.broadcast_to(x, (sub, b.shape[1])) for x in vals], axis=0)
    q_adj = (q * jnp.exp(b - per_block(starts))).astype(BF16)
    k_adj = (k * jnp.exp(per_block(ends) - b)).astype(BF16)
    adjacent = (blk_t == blk_s + 1) & (blk_t // 2 == blk_s // 2)
    mid = ends[nsub // 2 - 1]
    q_far = (q * jnp.exp(jnp.minimum(b - mid, 0.0))).astype(BF16)
    k_far = (k * jnp.exp(jnp.minimum(mid - b, 0.0))).astype(BF16)
    far = (row >= c // 2) & (col < c // 2)
    s = jnp.where(adjacent, _dot_nt(q_adj, k_adj), 0.0) + jnp.where(far, _dot_nt(q_far, k_far), 0.0)
    lane = lax.broadcasted_iota(jnp.int32, (sub, c), 1)
    tiles = []
    for i in range(nsub):
        q_blk, b_blk = q[sub * i:sub * (i + 1), :], b[sub * i:sub * (i + 1), :]
        tile = jnp.zeros((sub, c), F32)
        for key in range(sub * i, sub * (i + 1)):
            decay = jnp.exp(jnp.minimum(b_blk - b[key:key + 1, :], 0.0))
            column = jnp.sum(q_blk * decay * k[key:key + 1, :], axis=-1, keepdims=True)
            tile = jnp.where(lane == key, column, tile)
        tiles.append(tile)
    return s + jnp.where((blk_t == blk_s) & (col <= row), jnp.concatenate(tiles, axis=0), 0.0)


def _hgrn2_stages(p_ref, lb_ref, gain_ref, o_ref, st_ref, first_chunk):
    c = HG_CHUNK
    w = HG_WIDTH
    row = lax.broadcasted_iota(jnp.int32, (c, c), 0)
    col = lax.broadcasted_iota(jnp.int32, (c, c), 1)
    tril = col <= row
    cum = tril.astype(BF16)
    lb = lb_ref[...]
    gain = gain_ref[...]
    heads = [slice(h * HG_HEAD_DIM, (h + 1) * HG_HEAD_DIM) for h in range(HG_HEADS)]
    states = [[st_ref[h] for h in range(HG_HEADS)]] + [None] * HG_CHUNKS_PER_STEP

    def chunk_stages(rows, u):
        q_raw = p_ref[rows, 0:w]
        z = p_ref[rows, w:2 * w]
        v = p_ref[rows, 2 * w:3 * w]
        g_raw = p_ref[rows, 3 * w:4 * w]
        q = q_raw * jax.nn.sigmoid(q_raw)
        log_f = jnp.log(lb + (1.0 - lb) * jax.nn.sigmoid(z))
        k = (1.0 - lb) * jax.nn.sigmoid(-z)
        b = _dot_sel_lhs(cum, log_f)
        yield
        b_end = b[c - 1:c, :]
        q_st = (q * jnp.exp(b)).astype(BF16)
        k_st = (k * jnp.exp(b_end - b)).astype(BF16)
        d_end = jnp.exp(b_end)
        vb = v.astype(BF16)
        gate = g_raw * jax.nn.sigmoid(g_raw)
        scores = [_hgrn2_scores(q[:, sl], k[:, sl], b[:, sl]).astype(BF16) for sl in heads]
        updates = [_dot_tn(vb[:, sl], k_st[:, sl]) for sl in heads]
        yield
        st = states[u]
        states[u + 1] = [st[h] * d_end[:, sl] + updates[h] for h, sl in enumerate(heads)]
        outs = [_dot_nt(q_st[:, sl], st[h].astype(BF16)) + _dot(scores[h], vb[:, sl]) for h, sl in enumerate(heads)]
        yield
        for h, sl in enumerate(heads):
            o_ref[rows, sl] = (_rms(outs[h], HG_HEAD_DIM) * gain * gate[:, sl]).astype(o_ref.dtype)

    yield from _rounds([chunk_stages(pl.ds(pl.multiple_of((first_chunk + u) * c, c), c), u)
                        for u in range(HG_CHUNKS_PER_STEP)])
    for h in range(HG_HEADS):
        st_ref[h] = states[HG_CHUNKS_PER_STEP][h]


def _rope(x, cos, sin):
    half = MLA_ROPE // 2
    lane = lax.broadcasted_iota(jnp.int32, x.shape, 1)
    rot = jnp.where(lane < half, -pltpu.roll(x, LANES - half, 1), pltpu.roll(x, half, 1))
    return x * cos + rot * sin


def _mla_stages(p_ref, cos_ref, sin_ref, gqa_ref, wqb_ref, gkva_ref, wkvb_ref, gq_ref, gk_ref, q_ref, k_ref, v_ref):
    p = p_ref[...]
    cos = cos_ref[...]
    sin = sin_ref[...]
    scale = MLA_QK_DIM ** -0.5
    qn = (_rms(p[:, 0:MLA_Q_RANK], MLA_Q_RANK) * gqa_ref[...]).astype(BF16)
    q = _dot(qn, wqb_ref[...])
    kvn = (_rms(p[:, MLA_Q_RANK:MLA_Q_RANK + MLA_KV_RANK], MLA_KV_RANK) * gkva_ref[...]).astype(BF16)
    kv = _dot(kvn, wkvb_ref[...])
    gq_nope, gq_rope = gq_ref[0:1, :], gq_ref[1:2, :]
    gk_nope, gk_rope = gk_ref[0:1, :], gk_ref[1:2, :]
    k_rope = _rope(_rms(p[:, MLA_Q_RANK + MLA_KV_RANK:], MLA_ROPE) * gk_rope, cos, sin).astype(BF16)
    half = MLA_ROPE // 2
    first_half = lax.broadcasted_iota(jnp.int32, cos.shape, 1) < half
    yield

    def head_stages(h):
        o = h * MLA_QK_PAD
        q_nope, q_rope, k_nope = q[:, o:o + MLA_NOPE], q[:, o + MLA_NOPE:o + MLA_QK_PAD], kv[:, o:o + MLA_NOPE]
        ss_qn = jnp.sum(q_nope * q_nope, axis=-1, keepdims=True)
        ss_qr = jnp.sum(q_rope * q_rope, axis=-1, keepdims=True)
        ss_kn = jnp.sum(k_nope * k_nope, axis=-1, keepdims=True)
        yield
        q_nope = q_nope * lax.rsqrt(ss_qn * (1.0 / MLA_NOPE) + RMS_EPS) * (gq_nope * scale)
        q_rope = q_rope * lax.rsqrt(ss_qr * (1.0 / MLA_ROPE) + RMS_EPS) * (gq_rope * scale)
        k_nope = k_nope * lax.rsqrt(ss_kn * (1.0 / MLA_NOPE) + RMS_EPS) * gk_nope
        up, down = pltpu.roll(q_rope, LANES - half, 1), pltpu.roll(q_rope, half, 1)
        yield
        q_ref[h, :, 0:MLA_NOPE] = q_nope.astype(BF16)
        q_ref[h, :, MLA_NOPE:MLA_QK_PAD] = (q_rope * cos + jnp.where(first_half, -up, down) * sin).astype(BF16)
        k_ref[h, :, 0:MLA_NOPE] = k_nope.astype(BF16)
        k_ref[h, :, MLA_NOPE:MLA_QK_PAD] = k_rope
        v_ref[h] = kv[:, o + MLA_NOPE:o + MLA_QK_PAD].astype(BF16)

    yield from _rounds([head_stages(h) for h in range(MLA_HEADS)])


def _mixer_in_body(blocks_per_seq, x_ref, g_ref, whg_ref, wmla_ref, wrw_ref,
                   cos_ref, sin_ref, gqa_ref, wqb_ref, gkva_ref, wkvb_ref, gq_ref, gk_ref, lb_ref, hgain_ref,
                   prw_ref, q_ref, k_ref, v_ref, ohg_ref,
                   h_s, phg_s, pmla_s, st_ref):
    @pl.when(pl.program_id(0) % blocks_per_seq == 0)
    def _():
        st_ref[...] = jnp.zeros_like(st_ref)

    h_s[...] = (_rms(x_ref[...], x_ref.shape[-1]) * g_ref[...]).astype(BF16)

    def projections():
        pmla_s[...] = _dot(h_s[...], wmla_ref[...])
        yield
        for c0 in range(0, whg_ref.shape[1], PROJ_COLS):
            phg_s[:, c0:c0 + PROJ_COLS] = _dot(h_s[...], whg_ref[:, c0:c0 + PROJ_COLS])
            yield
        for c0 in range(0, wrw_ref.shape[1], PROJ_COLS):
            c1 = min(c0 + PROJ_COLS, wrw_ref.shape[1])
            prw_ref[:, c0:c1] = _dot(h_s[...], wrw_ref[:, c0:c1])
            yield

    def hgrn2_blocks():
        for first in range(0, x_ref.shape[0] // HG_CHUNK, HG_CHUNKS_PER_STEP):
            yield from _hgrn2_stages(phg_s, lb_ref, hgain_ref, ohg_ref, st_ref, first)

    proj = projections()
    next(proj)
    _alongside(proj, whg_ref.shape[1] // PROJ_COLS,
               _mla_stages(pmla_s, cos_ref, sin_ref, gqa_ref, wqb_ref, gkva_ref, wkvb_ref, gq_ref, gk_ref,
                           q_ref, k_ref, v_ref))
    _alongside(proj, pl.cdiv(wrw_ref.shape[1], PROJ_COLS), hgrn2_blocks())


def mixer_in(x, gain, w_hg, w_mla, w_rw, cos, sin, g_qa, w_qb, g_kva, w_kvb, g_q, g_k, lower_bound, out_gain,
             *, batch, bm):
    t, d = x.shape
    assert t % (batch * bm) == 0 and bm % (HG_CHUNK * HG_CHUNKS_PER_STEP) == 0
    resident = lambda a: pl.BlockSpec(a.shape, lambda i: (0,) * a.ndim, pipeline_mode=pl.Buffered(1))
    rows = lambda width: pl.BlockSpec((bm, width), lambda i: (i, 0))
    heads = lambda width: pl.BlockSpec((MLA_HEADS, bm, width), lambda i: (0, i, 0))
    params = [gain.reshape(1, d), w_hg, w_mla, w_rw]
    mla_params = [g_qa.reshape(1, -1), w_qb, g_kva.reshape(1, -1), w_kvb, g_q, g_k]
    hg_params = [lower_bound.reshape(1, HG_WIDTH), out_gain.reshape(1, HG_HEAD_DIM)]
    return pl.pallas_call(
        functools.partial(_mixer_in_body, t // batch // bm),
        grid=(t // bm,),
        in_specs=([rows(d)] + [resident(a) for a in params] + [rows(LANES), rows(LANES)]
                  + [resident(a) for a in mla_params + hg_params]),
        out_specs=[rows(w_rw.shape[1]), heads(MLA_QK_PAD), heads(MLA_QK_PAD), heads(MLA_V), rows(HG_WIDTH)],
        out_shape=[jax.ShapeDtypeStruct((t, w_rw.shape[1]), F32),
                   jax.ShapeDtypeStruct((MLA_HEADS, t, MLA_QK_PAD), BF16),
                   jax.ShapeDtypeStruct((MLA_HEADS, t, MLA_QK_PAD), BF16),
                   jax.ShapeDtypeStruct((MLA_HEADS, t, MLA_V), BF16),
                   jax.ShapeDtypeStruct((t, HG_WIDTH), BF16)],
        scratch_shapes=[pltpu.VMEM((bm, d), BF16), pltpu.VMEM((bm, w_hg.shape[1]), F32),
                        pltpu.VMEM((bm, w_mla.shape[1]), F32),
                        pltpu.VMEM((HG_HEADS, HG_HEAD_DIM, HG_HEAD_DIM), F32)],
        compiler_params=_cparams(("arbitrary",), 56),
        name="mixer_in",
    )(x, *params, cos, sin, *mla_params, *hg_params)


def _pipelined(stages):
    n = len(stages)
    done = [False] * n
    rnd = 0
    while not all(done):
        for i in range(min(rnd, n - 1) + 1):
            if not done[i]:
                try:
                    next(stages[i])
                except StopIteration:
                    done[i] = True
        rnd += 1


def _flash_body(q_ref, k_ref, v_ref, o_ref):
    i = pl.program_id(2)
    blk = q_ref.shape[1]
    rb = blk // ATTN_ROW_CHAINS

    def unit(state, r, j, diagonal):
        rows = slice(r * rb, (r + 1) * rb)
        ncol = (r + 1) * rb if diagonal else blk
        keys = slice(j * blk, j * blk + ncol)
        s = _dot_nt(q_ref[0, rows, :], k_ref[0, keys, :])
        yield
        if diagonal:
            row = lax.broadcasted_iota(jnp.int32, (rb, rb), 0)
            col = lax.broadcasted_iota(jnp.int32, (rb, rb), 1)
            last = jnp.where(col <= row, s[:, r * rb:], MASK_VALUE)
            s = jnp.concatenate([s[:, :r * rb], last], axis=1) if r else last
        m_old, l_old, acc_old = state[r]
        m_new = jnp.maximum(m_old, jnp.max(s, axis=-1, keepdims=True))
        alpha = jnp.exp(m_old - m_new)
        p = jnp.exp(s - jnp.concatenate([m_new] * (ncol // LANES), axis=1))
        l_new = alpha * l_old + jnp.sum(p, axis=-1, keepdims=True)
        pb = p.astype(BF16)
        yield
        acc = alpha * acc_old + _dot(pb, v_ref[0, keys, :])
        if diagonal:
            o_ref[rows, :] = (acc / l_new).astype(o_ref.dtype)
        else:
            state[r] = (m_new, l_new, acc)

    def query_block(nfull):
        start = (jnp.full((rb, LANES), MASK_VALUE, F32), jnp.zeros((rb, LANES), F32), jnp.zeros((rb, MLA_V), F32))
        state = [start] * ATTN_ROW_CHAINS
        units = [unit(state, r, j, False) for j in range(nfull) for r in range(ATTN_ROW_CHAINS)]
        units += [unit(state, r, nfull, True) for r in range(ATTN_ROW_CHAINS)]
        _pipelined(units)

    for nfull in range(k_ref.shape[1] // blk):
        pl.when(i == nfull)(functools.partial(query_block, nfull))


def flash_attn(q, k, v, *, batch, blk):
    heads, t, _ = q.shape
    seq = t // batch
    nb = seq // blk
    assert t == batch * nb * blk and blk % (ATTN_ROW_CHAINS * LANES) == 0
    kv = lambda width: pl.BlockSpec((1, seq, width), lambda b, h, i: (h, b, 0))
    return pl.pallas_call(
        _flash_body,
        grid=(batch, heads, nb),
        in_specs=[pl.BlockSpec((1, blk, MLA_QK_PAD), lambda b, h, i: (h, b * nb + i, 0)), kv(MLA_QK_PAD), kv(MLA_V)],
        out_specs=pl.BlockSpec((blk, MLA_V), lambda b, h, i: (b * nb + i, h)),
        out_shape=jax.ShapeDtypeStruct((t, heads * MLA_V), BF16),
        compiler_params=_cparams(("parallel", "parallel", "arbitrary"), 48),
        name="flash_attn",
    )(q, k, v)


def _head_sums(x, weight):
    half = 2 * LANES
    gi = lax.broadcasted_iota(jnp.int32, (half, half), 0) // RW_HEAD
    gj = lax.broadcasted_iota(jnp.int32, (half, half), 1) // RW_HEAD
    sel = jnp.where(gi == gj, weight, 0.0).astype(BF16)
    return jnp.concatenate([_dot_sel_rhs(x[:, 0:half], sel, 2), _dot_sel_rhs(x[:, half:], sel, 2)], axis=1)


def _rounds(stages):
    results = [None] * len(stages)
    live = list(range(len(stages)))
    while live:
        for i in list(live):
            try:
                next(stages[i])
            except StopIteration as stop:
                results[i] = stop.value
                live.remove(i)
        yield
    return results


def _lock_step(stages):
    rounds = _rounds(stages)
    while True:
        try:
            next(rounds)
        except StopIteration as stop:
            return stop.value


def _alongside(main, steps, side):
    for _ in range(steps):
        next(main)
        next(side, None)
    for _ in side:
        pass


def _unit_lower_inverse(a, block):
    n = a.shape[0]
    row = lax.broadcasted_iota(jnp.int32, (n, n), 0)
    col = lax.broadcasted_iota(jnp.int32, (n, n), 1)
    diff = row ^ col
    t = jnp.where(row == col, 1.0, 0.0) + jnp.where(diff < 2, a, 0.0)
    size = 2
    while size < block:
        off = jnp.where((diff >= size) & (diff < 2 * size), a, 0.0).astype(BF16)
        tb = t.astype(BF16)
        half = _dot(tb, off).astype(BF16)
        yield
        t = t + _dot(half, tb)
        yield
        size *= 2
    return t


def _rw_chunk_body(has_vres, *refs):
    if has_vres:
        (p_ref, prev_ref, mu_ref, w0_ref, w2_ref, a0_ref, a2_ref, g2_ref, kk_ref, ka_ref, rk_ref,
         v0_ref, v1_ref, v2_ref, vf_ref,
         m_ref, nt_ref, dec_ref, qh_ref, oh_ref, gate_ref, bonus_ref,
         r_s, g_s, k_s, v_s, kk_s, b_s) = refs
    else:
        (p_ref, prev_ref, mu_ref, w0_ref, w2_ref, a0_ref, a2_ref, g2_ref, kk_ref, ka_ref, rk_ref,
         m_ref, nt_ref, dec_ref, qh_ref, oh_ref, gate_ref, bonus_ref, vout_ref,
         r_s, g_s, k_s, v_s, kk_s, b_s) = refs
    c = RW_CHUNK
    w = RW_WIDTH
    bt = p_ref.shape[0]

    p = p_ref[...]
    first = pl.program_id(1) == 0
    prev_row = jnp.where(first, 0.0, prev_ref[7:8, :])
    rowi = lax.broadcasted_iota(jnp.int32, p.shape, 0)
    prev = jnp.where(rowi == 0, prev_row, pltpu.roll(p, 1, 0))
    p = p + (prev - p) * mu_ref[...]
    r = p[:, 0:w]
    k = p[:, w:2 * w]
    v = p[:, 2 * w:3 * w]
    wd = p[:, 3 * w:3 * w + RW_W_RANK]
    ad = p[:, 3 * w + RW_W_RANK:3 * w + RW_W_RANK + RW_A_RANK]
    gd = p[:, 3 * w + RW_W_RANK + RW_A_RANK:]
    w_log = -jax.nn.softplus(-(w0_ref[...] + _dot(jnp.tanh(wd).astype(BF16), w2_ref[...]))) - 0.5
    lr = jax.nn.sigmoid(a0_ref[...] + _dot(ad.astype(BF16), a2_ref[...]))
    gate = _dot(jax.nn.sigmoid(gd).astype(BF16), g2_ref[...])
    if has_vres:
        mix = jax.nn.sigmoid(v0_ref[...] + _dot(_dot(v.astype(BF16), v1_ref[...]).astype(BF16), v2_ref[...]))
        v = v + (vf_ref[...] - v) * mix
    else:
        vout_ref[...] = v
    kk = k * kk_ref[...]
    kk = kk / jnp.maximum(jnp.sqrt(_head_sums(kk * kk, 1.0)), 1e-12)
    k = k * (1.0 + (lr - 1.0) * ka_ref[...])
    gate_ref[...] = gate
    bonus_ref[...] = _head_sums(r * k * rk_ref[...], 1.0) * v
    r_s[...] = r
    g_s[...] = -jnp.exp(w_log)
    k_s[...] = k
    v_s[...] = v
    kk_s[...] = kk
    b_s[...] = kk * lr

    n2 = 2 * c
    row = lax.broadcasted_iota(jnp.int32, (n2, n2), 0)
    col = lax.broadcasted_iota(jnp.int32, (n2, n2), 1)
    same_head = (row ^ col) < c
    incl = same_head & (col <= row)
    strict = same_head & (col < row)
    ci_row = lax.broadcasted_iota(jnp.int32, (c, c), 0)
    ci_col = lax.broadcasted_iota(jnp.int32, (c, c), 1)
    cum = (ci_col <= ci_row).astype(BF16)
    lane = lax.broadcasted_iota(jnp.int32, (1, LANES), 1)
    head0 = lane < RW_HEAD

    def stack(z):
        return jnp.concatenate([jnp.where(head0, z, 0.0), jnp.where(head0, 0.0, z)], axis=0)

    def pair_matrices(g, r, k, v, kkc, bc):
        gc = _dot_sel_lhs(cum, g)
        yield
        g_end = gc[c - 1:c, :]
        inv = jnp.exp(-gc)
        to_end = jnp.exp(g_end - gc)
        a2 = stack(-kkc * jnp.exp(gc - g)).astype(BF16)
        r2 = stack(r * jnp.exp(gc))
        v2 = stack(v).astype(BF16)
        b_end = stack(bc * to_end).astype(BF16)
        k_end = stack(k * to_end).astype(BF16)
        b_inv = (bc * inv).astype(BF16)
        k_inv = (k * inv).astype(BF16)
        ar = jnp.concatenate([a2, r2.astype(BF16)], axis=0)
        abk = _dot_nt(ar, jnp.concatenate([b_inv, b_inv, k_inv, k_inv], axis=0))
        yield
        a_ab = jnp.where(strict, abk[0:n2, 0:n2], 0.0)
        a_rb = jnp.where(incl, abk[n2:2 * n2, 0:n2], 0.0).astype(BF16)
        a_ak = jnp.where(strict, abk[0:n2, n2:2 * n2], 0.0).astype(BF16)
        a_rk = jnp.where(incl, abk[n2:2 * n2, n2:2 * n2], 0.0).astype(BF16)
        x = _dot(a_ak, v2).astype(BF16)
        nt = _dot_tn(v2, k_end)
        oh2 = _dot(a_rk, v2)
        t_inv = (yield from _unit_lower_inverse(a_ab, c)).astype(BF16)
        p12 = _dot(t_inv, jnp.concatenate([a2, x], axis=1)).astype(BF16)
        yield
        qo = _dot(a_rb, p12)
        qh2 = r2 + qo[:, 0:LANES]
        oh2 = oh2 + qo[:, LANES:2 * LANES]
        m = _dot_tn(b_end, p12[:, 0:LANES])
        nt = nt + _dot_tn(p12[:, LANES:2 * LANES], b_end)
        return m, nt, jnp.exp(g_end), qh2[0:c] + qh2[c:n2], oh2[0:c] + oh2[c:n2]

    def chunk(ci, carry):
        units = [(ci * RW_CHUNKS_PER_STEP + u, pr, slice(pr * LANES, (pr + 1) * LANES))
                 for u in range(RW_CHUNKS_PER_STEP) for pr in range(RW_PAIRS)]
        rows = lambda cj: pl.ds(pl.multiple_of(cj * c, c), c)
        loaded = [[s[rows(cj), sl] for s in (g_s, r_s, k_s, v_s, kk_s, b_s)] for cj, _, sl in units]
        results = _lock_step([pair_matrices(*operands) for operands in loaded])
        for (cj, pr, sl), (m, nt, dec, qh, oh) in zip(units, results):
            m_ref[cj, pr] = m.astype(m_ref.dtype)
            nt_ref[cj, pr] = nt
            dec_ref[cj, pr] = dec
            qh_ref[rows(cj), sl] = qh.astype(qh_ref.dtype)
            oh_ref[rows(cj), sl] = oh
        return carry

    lax.fori_loop(0, bt // c // RW_CHUNKS_PER_STEP, chunk, 0)


def rw_chunk(p_rw, mu, w0, w2, a0, a2, g2, k_k, k_a, r_k, vres, *, batch, bt):
    t = p_rw.shape[0]
    nb = t // batch // bt
    nc = bt // RW_CHUNK
    assert t == batch * nb * bt and bt % (RW_CHUNK * RW_CHUNKS_PER_STEP) == 0
    has_vres = vres is not None
    row1 = lambda a: a.reshape(1, -1)
    full = lambda a: pl.BlockSpec(a.shape, lambda b, s: (0,) * a.ndim)
    rows = lambda width: pl.BlockSpec((bt, width), lambda b, s: (b * nb + s, 0))
    prev_spec = pl.BlockSpec((8, RW_IN), lambda b, s: (jnp.maximum((b * nb + s) * (bt // 8) - 1, 0), 0))
    mats = pl.BlockSpec((nc, RW_PAIRS, LANES, LANES), lambda b, s: (b * nb + s, 0, 0, 0))
    params = [row1(mu), row1(w0), w2, row1(a0), a2, g2, row1(k_k), row1(k_a), row1(r_k)]
    args = [p_rw, p_rw] + params
    in_specs = [rows(RW_IN), prev_spec] + [full(a) for a in params]
    if has_vres:
        v0, v1, v2, v_first = vres
        extra = [row1(v0), v1, v2]
        args += extra + [v_first]
        in_specs += [full(a) for a in extra] + [rows(RW_WIDTH)]
    decs = pl.BlockSpec((nc, RW_PAIRS, 1, LANES), lambda b, s: (b * nb + s, 0, 0, 0))
    mat_shape = lambda dt: jax.ShapeDtypeStruct((t // RW_CHUNK, RW_PAIRS, LANES, LANES), dt)
    dec_shape = jax.ShapeDtypeStruct((t // RW_CHUNK, RW_PAIRS, 1, LANES), F32)
    tok_shape = lambda dt: jax.ShapeDtypeStruct((t, RW_WIDTH), dt)
    out_specs = [mats, mats, decs, rows(RW_WIDTH), rows(RW_WIDTH), rows(RW_WIDTH), rows(RW_WIDTH)]
    out_shape = [mat_shape(BF16), mat_shape(F32), dec_shape, tok_shape(BF16), tok_shape(F32), tok_shape(F32),
                 tok_shape(F32)]
    if not has_vres:
        out_specs.append(rows(RW_WIDTH))
        out_shape.append(tok_shape(F32))
    return pl.pallas_call(
        functools.partial(_rw_chunk_body, has_vres),
        grid=(batch, nb),
        in_specs=in_specs,
        out_specs=out_specs,
        out_shape=out_shape,
        scratch_shapes=[pltpu.VMEM((bt, RW_WIDTH), F32) for _ in range(6)],
        compiler_params=_cparams(("parallel", "arbitrary"), 48),
        name="rw_chunk",
    )(*args)


def _rw_scan_stages(m_ref, nt_ref, dec_ref, qh_ref, oh_ref, gate_ref, bonus_ref, lnw_ref, lnb_ref, o_ref, y_ref, h_ref):
    c = RW_CHUNK
    states = [h_ref[pr] for pr in range(RW_PAIRS)]
    for ci in range(qh_ref.shape[0] // c):
        rows = slice(ci * c, (ci + 1) * c)
        for pr in range(RW_PAIRS):
            sl = slice(pr * LANES, (pr + 1) * LANES)
            ht = states[pr]
            hb = ht.astype(BF16)
            y_ref[rows, sl] = _dot_nt(qh_ref[rows, sl], hb) + oh_ref[rows, sl]
            states[pr] = ht * dec_ref[ci, pr] + _dot_nt(hb, m_ref[ci, pr]) + nt_ref[ci, pr]
        yield
    for pr in range(RW_PAIRS):
        h_ref[pr] = states[pr]
    y = y_ref[...]
    mean = _head_sums(y, 1.0 / RW_HEAD)
    d = y - mean
    var = _head_sums(d * d, 1.0 / RW_HEAD)
    y = d * lax.rsqrt(var + RW_GN_EPS) * lnw_ref[...] + lnb_ref[...]
    o_ref[...] = ((y + bonus_ref[...]) * gate_ref[...]).astype(o_ref.dtype)


def _out_mix_body(blocks_per_seq, x_ref, ohg_ref, omla_ref, whg_ref, wmla_ref, wrw_ref,
                  m_ref, nt_ref, dec_ref, qh_ref, oh_ref, gate_ref, bonus_ref, lnw_ref, lnb_ref,
                  o_ref, orw_s, y_s, h_s):
    @pl.when(pl.program_id(0) % blocks_per_seq == 0)
    def _():
        h_s[...] = jnp.zeros_like(h_s)

    d = x_ref.shape[1]

    def projections():
        for c0 in range(0, d, OUT_COLS):
            cols = slice(c0, c0 + OUT_COLS)
            o_ref[:, cols] = (x_ref[:, cols] + _dot(ohg_ref[...], whg_ref[:, cols])
                              + _dot(omla_ref[...], wmla_ref[:, cols]))
            yield

    scan = _rw_scan_stages(m_ref, nt_ref, dec_ref, qh_ref, oh_ref, gate_ref, bonus_ref, lnw_ref, lnb_ref,
                           orw_s, y_s, h_s)
    _alongside(projections(), d // OUT_COLS, scan)
    o_ref[...] += _dot(orw_s[...], wrw_ref[...])


def out_mix(x, o_hg, o_mla, w_hg, w_mla, w_rw, m, nt, dec, qh, oh, gate, bonus, ln_w, ln_b, *, batch, bs):
    t, d = x.shape
    nc = bs // RW_CHUNK
    assert t % (batch * bs) == 0 and d // OUT_COLS >= nc
    row = lambda width: pl.BlockSpec((bs, width), lambda i: (i, 0))
    resident = lambda a: pl.BlockSpec(a.shape, lambda i: (0,) * a.ndim, pipeline_mode=pl.Buffered(1))
    mats = pl.BlockSpec((nc, RW_PAIRS, LANES, LANES), lambda i: (i, 0, 0, 0))
    decs = pl.BlockSpec((nc, RW_PAIRS, 1, LANES), lambda i: (i, 0, 0, 0))
    ln_w, ln_b = ln_w.reshape(1, -1), ln_b.reshape(1, -1)
    return pl.pallas_call(
        functools.partial(_out_mix_body, t // batch // bs),
        grid=(t // bs,),
        in_specs=[row(d), row(HG_WIDTH), row(MLA_WIDTH), resident(w_hg), resident(w_mla), resident(w_rw),
                  mats, mats, decs, row(RW_WIDTH), row(RW_WIDTH), row(RW_WIDTH), row(RW_WIDTH),
                  resident(ln_w), resident(ln_b)],
        out_specs=row(d),
        out_shape=jax.ShapeDtypeStruct((t, d), F32),
        scratch_shapes=[pltpu.VMEM((bs, RW_WIDTH), BF16), pltpu.VMEM((bs, RW_WIDTH), F32),
                        pltpu.VMEM((RW_PAIRS, LANES, LANES), F32)],
        compiler_params=_cparams(("arbitrary",), 56),
        name="out_mix",
    )(x, o_hg, o_mla, w_hg, w_mla, w_rw, m, nt, dec, qh, oh, gate, bonus, ln_w, ln_b)


def _rope_tables(positions):
    half = MLA_ROPE // 2
    inv_freq = ROPE_THETA ** (-jnp.arange(half, dtype=F32) / half)
    ang = positions.astype(F32).reshape(-1, 1) * inv_freq
    zeros = jnp.zeros((ang.shape[0], LANES - MLA_ROPE), F32)
    cos = jnp.concatenate([jnp.cos(ang), jnp.cos(ang), zeros], axis=1)
    sin = jnp.concatenate([jnp.sin(ang), jnp.sin(ang), zeros], axis=1)
    return cos, sin


def _pad_cols(a, width):
    return jnp.pad(a, [(0, 0)] * (a.ndim - 1) + [(0, width - a.shape[-1])])


def _forward(x, positions, attn_norm, w_in, hg_lower_bounds, hg_out_norm,
             mla_q_a_norm, mla_w_qb, mla_kv_a_norm, mla_w_kvb, mla_q_norm, mla_k_norm,
             rw_mu, rw_w0, rw_w2, rw_a0, rw_a2, rw_g2, rw_v0, rw_v1, rw_v2,
             rw_k_k, rw_k_a, rw_r_k, rw_ln_w, rw_ln_b,
             w_o, ffn_norm, w_gate_up, w_down, *, cfg):
    batch, seq, d = x.shape
    depth = w_in.shape[0]
    t = batch * seq
    x = x.reshape(t, d)

    lb_sm = jax.nn.softmax(hg_lower_bounds.astype(F32), axis=0)
    lower_bounds = jnp.cumsum(lb_sm, axis=0) - lb_sm[0]
    cos, sin = _rope_tables(positions)

    hg_end = 4 * HG_WIDTH
    mla_end = hg_end + MLA_IN
    w_in_hg = w_in[:, :, :hg_end].astype(BF16)
    w_in_mla = _pad_cols(w_in[:, :, hg_end:mla_end], MLA_IN_PAD).astype(BF16)
    w_in_rw = w_in[:, :, mla_end:].astype(BF16)
    w_qb = _pad_cols(mla_w_qb.reshape(depth, MLA_Q_RANK, MLA_HEADS, MLA_QK_DIM), MLA_QK_PAD)
    w_qb = w_qb.reshape(depth, MLA_Q_RANK, MLA_HEADS * MLA_QK_PAD).astype(BF16)
    w_kvb = mla_w_kvb.astype(BF16)
    split_gain = lambda g: jnp.stack([g[:, :MLA_NOPE], _pad_cols(g[:, MLA_NOPE:], LANES)], axis=1)
    g_q = split_gain(mla_q_norm)
    g_k = split_gain(mla_k_norm)
    w_o_hg = w_o[:, :HG_WIDTH].astype(BF16)
    w_o_mla = w_o[:, HG_WIDTH:HG_WIDTH + MLA_WIDTH].astype(BF16)
    w_o_rw = w_o[:, HG_WIDTH + MLA_WIDTH:].astype(BF16)
    w_gu = w_gate_up.astype(BF16)
    w_dn = w_down.astype(BF16)
    rw_w2b, rw_a2b, rw_g2b = rw_w2.astype(BF16), rw_a2.astype(BF16), rw_g2.astype(BF16)
    rw_v1b, rw_v2b = rw_v1.astype(BF16), rw_v2.astype(BF16)
    r_k = rw_r_k.reshape(depth, RW_WIDTH)

    v_first = None
    for l in range(depth):
        p_rw, q, k, v, o_hg = mixer_in(x, attn_norm[l], w_in_hg[l], w_in_mla[l], w_in_rw[l], cos, sin,
                                       mla_q_a_norm[l], w_qb[l], mla_kv_a_norm[l], w_kvb[l], g_q[l], g_k[l],
                                       lower_bounds[l], hg_out_norm[l], batch=batch, bm=cfg["proj_bm"])
        o_mla = flash_attn(q, k, v, batch=batch, blk=cfg["attn_blk"])

        vres = None if l == 0 else (rw_v0[l - 1], rw_v1b[l - 1], rw_v2b[l - 1], v_first)
        outs = rw_chunk(p_rw, rw_mu[l], rw_w0[l], rw_w2b[l], rw_a0[l], rw_a2b[l], rw_g2b[l],
                        rw_k_k[l], rw_k_a[l], r_k[l], vres, batch=batch, bt=cfg["rw_bt"])
        if l == 0:
            v_first = outs[7]
        x = out_mix(x, o_hg, o_mla, w_o_hg[l], w_o_mla[l], w_o_rw[l], *outs[:7], rw_ln_w[l], rw_ln_b[l],
                    batch=batch, bs=cfg["out_bm"])
        x = ffn(x, ffn_norm[l], w_gu[l], w_dn[l], bm=cfg["ffn_bm"], bf=cfg["ffn_bf"])
    return x.reshape(batch, seq, d)


_CFG = dict(proj_bm=256, attn_blk=1024, rw_bt=256,
            out_bm=512, ffn_bm=1024, ffn_bf=512)


def kernel(x, positions, attn_norm, w_in, hg_lower_bounds, hg_out_norm, mla_q_a_norm, mla_w_qb, mla_kv_a_norm,
           mla_w_kvb, mla_q_norm, mla_k_norm, rw_mu, rw_w0, rw_w2, rw_a0, rw_a2, rw_g2, rw_v0, rw_v1, rw_v2,
           rw_k_k, rw_k_a, rw_r_k, rw_ln_w, rw_ln_b, w_o, ffn_norm, w_gate_up, w_down):
    return _forward(x, positions, attn_norm, w_in, hg_lower_bounds, hg_out_norm, mla_q_a_norm, mla_w_qb,
                    mla_kv_a_norm, mla_w_kvb, mla_q_norm, mla_k_norm, rw_mu, rw_w0, rw_w2, rw_a0, rw_a2, rw_g2,
                    rw_v0, rw_v1, rw_v2, rw_k_k, rw_k_a, rw_r_k, rw_ln_w, rw_ln_b, w_o, ffn_norm, w_gate_up,
                    w_down, cfg=_CFG)
```

```python
import functools

import jax
import jax.numpy as jnp
from jax import lax
from jax.experimental import pallas as pl
from jax.experimental.pallas import tpu as pltpu

F32 = jnp.float32
BF16 = jnp.bfloat16

HG_WIDTH = 512
HG_HEAD_DIM = 128
HG_HEADS = HG_WIDTH // HG_HEAD_DIM
MLA_NOPE = 128
MLA_ROPE = 64
MLA_V = 128
MLA_HEADS = 8
MLA_WIDTH = MLA_HEADS * MLA_V
MLA_Q_RANK = 512
MLA_KV_RANK = 256
MLA_QK_DIM = MLA_NOPE + MLA_ROPE
MLA_QK_PAD = 256
MLA_IN = MLA_Q_RANK + MLA_KV_RANK + MLA_ROPE
MLA_IN_PAD = 896
ROPE_THETA = 10000.0
MASK_VALUE = -1e30
RW_WIDTH = 512
RW_HEAD = 64
RW_HEADS = RW_WIDTH // RW_HEAD
RW_PAIRS = RW_HEADS // 2
RW_W_RANK = 64
RW_A_RANK = 64
RW_G_RANK = 128
RW_IN = 3 * RW_WIDTH + RW_W_RANK + RW_A_RANK + RW_G_RANK
RW_GN_EPS = 64e-5
RMS_EPS = 1e-6

HG_CHUNK = 64
HG_SUB = 8
HG_CHUNKS_PER_STEP = 4
RW_CHUNK = 64
RW_CHUNKS_PER_STEP = 4
ATTN_ROW_CHAINS = 4
PROJ_COLS = 512
OUT_COLS = 256
LANES = 128

_MIB = 1024 * 1024


def _cparams(semantics, vmem_mib):
    return pltpu.CompilerParams(dimension_semantics=semantics, vmem_limit_bytes=vmem_mib * _MIB)


def _dot(a, b):
    return jnp.dot(a, b, preferred_element_type=F32)


def _dot_nt(a, b):
    return lax.dot_general(a, b, (((1,), (1,)), ((), ())), preferred_element_type=F32)


def _dot_tn(a, b):
    return lax.dot_general(a, b, (((0,), (0,)), ((), ())), preferred_element_type=F32)


def _split(x, terms):
    parts = []
    for _ in range(terms - 1):
        hi = x.astype(BF16)
        parts.append(hi)
        x = x - hi.astype(F32)
    parts.append(x.astype(BF16))
    return parts


def _dot_sel_lhs(sel, x, terms=3):
    return sum(_dot(sel, part) for part in _split(x, terms))


def _dot_sel_rhs(x, sel, terms=3):
    return sum(_dot(part, sel) for part in _split(x, terms))


def _rms(x, width):
    ms = jnp.sum(x * x, axis=-1, keepdims=True) * (1.0 / width)
    return x * lax.rsqrt(ms + RMS_EPS)


def _ffn_body(x_ref, g_ref, wg_ref, wu_ref, wd_ref, o_ref, h_ref):
    @pl.when(pl.program_id(1) == 0)
    def _():
        x = x_ref[...]
        h_ref[...] = (_rms(x, x.shape[-1]) * g_ref[...]).astype(BF16)
        o_ref[...] = x

    h = h_ref[...]
    gate = _dot(h, wg_ref[...])
    up = _dot(h, wu_ref[...])
    act = (gate * jax.nn.sigmoid(gate) * up).astype(BF16)
    o_ref[...] += _dot(act, wd_ref[...])


def ffn(x, gain, w_gate_up, w_down, *, bm, bf):
    t, d = x.shape
    dff = w_down.shape[0]
    nf = dff // bf
    return pl.pallas_call(
        _ffn_body,
        grid=(t // bm, nf),
        in_specs=[
            pl.BlockSpec((bm, d), lambda i, j: (i, 0)),
            pl.BlockSpec((1, d), lambda i, j: (0, 0)),
            pl.BlockSpec((d, bf), lambda i, j: (0, j)),
            pl.BlockSpec((d, bf), lambda i, j: (0, j + nf)),
            pl.BlockSpec((bf, d), lambda i, j: (j, 0)),
        ],
        out_specs=pl.BlockSpec((bm, d), lambda i, j: (i, 0)),
        out_shape=jax.ShapeDtypeStruct((t, d), F32),
        scratch_shapes=[pltpu.VMEM((bm, d), BF16)],
        compiler_params=_cparams(("parallel", "arbitrary"), 56),
        name="ffn",
    )(x, gain.reshape(1, d), w_gate_up, w_gate_up, w_down)


def _hgrn2_scores(q, k, b):
    c, sub = HG_CHUNK, HG_SUB
    nsub = c // sub
    row = lax.broadcasted_iota(jnp.int32, (c, c), 0)
    col = lax.broadcasted_iota(jnp.int32, (c, c), 1)
    blk_t, blk_s = row // sub, col // sub
    zero = jnp.zeros_like(b[0:1, :])
    s = jnp.zeros((c, c), F32)
    g = sub
    while g < c:
        ends = [b[g * i + g - 1:g * i + g, :] for i in range(c // g)]
        per_group = lambda vals: jnp.concatenate([jnp.broadcast_to(x, (g, b.shape[1])) for x in vals], axis=0)
        q_up = (q * jnp.exp(b - per_group([zero] + ends[:-1]))).astype(BF16)
        k_low = (k * jnp.exp(per_group(ends) - b)).astype(BF16)
        halves = (row // g == col // g + 1) & (row // (2 * g) == col // (2 * g))
        s = s + jnp.where(halves, _dot_nt(q_up, k_low), 0.0)
        g *= 2
    lane = lax.broadcasted_iota(jnp.int32, (sub, c), 1)
    tiles = []
    for i in range(nsub):
        q_blk, b_blk = q[sub * i:sub * (i + 1), :], b[sub * i:sub * (i + 1), :]
        tile = jnp.zeros((sub, c), F32)
        for key in range(sub * i, sub * (i + 1)):
            decay = jnp.exp(jnp.minimum(b_blk - b[key:key + 1, :], 0.0))
            column = jnp.sum(q_blk * decay * k[key:key + 1, :], axis=-1, keepdims=True)
            tile = jnp.where(lane == key, column, tile)
        tiles.append(tile)
    return s + jnp.where((blk_t == blk_s) & (col <= row), jnp.concatenate(tiles, axis=0), 0.0)


def _hgrn2_stages(p_ref, lb_ref, gain_ref, o_ref, st_ref, first_chunk):
    c = HG_CHUNK
    w = HG_WIDTH
    row = lax.broadcasted_iota(jnp.int32, (c, c), 0)
    col = lax.broadcasted_iota(jnp.int32, (c, c), 1)
    tril = col <= row
    cum = tril.astype(BF16)
    lb = lb_ref[...]
    gain = gain_ref[...]
    heads = [slice(h * HG_HEAD_DIM, (h + 1) * HG_HEAD_DIM) for h in range(HG_HEADS)]
    states = [[st_ref[h] for h in range(HG_HEADS)]] + [None] * HG_CHUNKS_PER_STEP

    def chunk_stages(rows, u):
        q_raw = p_ref[rows, 0:w]
        z = p_ref[rows, w:2 * w]
        v = p_ref[rows, 2 * w:3 * w]
        g_raw = p_ref[rows, 3 * w:4 * w]
        q = q_raw * jax.nn.sigmoid(q_raw)
        log_f = jnp.log(lb + (1.0 - lb) * jax.nn.sigmoid(z))
        k = (1.0 - lb) * jax.nn.sigmoid(-z)
        b = _dot_sel_lhs(cum, log_f)
        yield
        b_end = b[c - 1:c, :]
        q_st = (q * jnp.exp(b)).astype(BF16)
        k_st = (k * jnp.exp(b_end - b)).astype(BF16)
        d_end = jnp.exp(b_end)
        vb = v.astype(BF16)
        gate = g_raw * jax.nn.sigmoid(g_raw)
        scores = [_hgrn2_scores(q[:, sl], k[:, sl], b[:, sl]).astype(BF16) for sl in heads]
        updates = [_dot_tn(vb[:, sl], k_st[:, sl]) for sl in heads]
        yield
        st = states[u]
        states[u + 1] = [st[h] * d_end[:, sl] + updates[h] for h, sl in enumerate(heads)]
        outs = [_dot_nt(q_st[:, sl], st[h].astype(BF16)) + _dot(scores[h], vb[:, sl]) for h, sl in enumerate(heads)]
        yield
        for h, sl in enumerate(heads):
            o_ref[rows, sl] = (_rms(outs[h], HG_HEAD_DIM) * gain * gate[:, sl]).astype(o_ref.dtype)

    yield from _rounds([chunk_stages(pl.ds(pl.multiple_of((first_chunk + u) * c, c), c), u)
                        for u in range(HG_CHUNKS_PER_STEP)])
    for h in range(HG_HEADS):
        st_ref[h] = states[HG_CHUNKS_PER_STEP][h]


def _rope(x, cos, sin):
    half = MLA_ROPE // 2
    lane = lax.broadcasted_iota(jnp.int32, x.shape, 1)
    rot = jnp.where(lane < half, -pltpu.roll(x, LANES - half, 1), pltpu.roll(x, half, 1))
    return x * cos + rot * sin


def _mla_stages(p_ref, cos_ref, sin_ref, gqa_ref, wqb_ref, gkva_ref, wkvb_ref, gq_ref, gk_ref, q_ref, k_ref, v_ref):
    p = p_ref[...]
    cos = cos_ref[...]
    sin = sin_ref[...]
    scale = MLA_QK_DIM ** -0.5
    qn = (_rms(p[:, 0:MLA_Q_RANK], MLA_Q_RANK) * gqa_ref[...]).astype(BF16)
    q = _dot(qn, wqb_ref[...])
    kvn = (_rms(p[:, MLA_Q_RANK:MLA_Q_RANK + MLA_KV_RANK], MLA_KV_RANK) * gkva_ref[...]).astype(BF16)
    kv = _dot(kvn, wkvb_ref[...])
    gq_nope, gq_rope = gq_ref[0:1, :], gq_ref[1:2, :]
    gk_nope, gk_rope = gk_ref[0:1, :], gk_ref[1:2, :]
    k_rope = _rope(_rms(p[:, MLA_Q_RANK + MLA_KV_RANK:], MLA_ROPE) * gk_rope, cos, sin).astype(BF16)
    half = MLA_ROPE // 2
    first_half = lax.broadcasted_iota(jnp.int32, cos.shape, 1) < half
    yield

    def head_stages(h):
        o = h * MLA_QK_PAD
        q_nope, q_rope, k_nope = q[:, o:o + MLA_NOPE], q[:, o + MLA_NOPE:o + MLA_QK_PAD], kv[:, o:o + MLA_NOPE]
        ss_qn = jnp.sum(q_nope * q_nope, axis=-1, keepdims=True)
        ss_qr = jnp.sum(q_rope * q_rope, axis=-1, keepdims=True)
        ss_kn = jnp.sum(k_nope * k_nope, axis=-1, keepdims=True)
        yield
        q_nope = q_nope * lax.rsqrt(ss_qn * (1.0 / MLA_NOPE) + RMS_EPS) * (gq_nope * scale)
        q_rope = q_rope * lax.rsqrt(ss_qr * (1.0 / MLA_ROPE) + RMS_EPS) * (gq_rope * scale)
        k_nope = k_nope * lax.rsqrt(ss_kn * (1.0 / MLA_NOPE) + RMS_EPS) * gk_nope
        up, down = pltpu.roll(q_rope, LANES - half, 1), pltpu.roll(q_rope, half, 1)
        yield
        q_ref[h, :, 0:MLA_NOPE] = q_nope.astype(BF16)
        q_ref[h, :, MLA_NOPE:MLA_QK_PAD] = (q_rope * cos + jnp.where(first_half, -up, down) * sin).astype(BF16)
        k_ref[h, :, 0:MLA_NOPE] = k_nope.astype(BF16)
        k_ref[h, :, MLA_NOPE:MLA_QK_PAD] = k_rope
        v_ref[h] = kv[:, o + MLA_NOPE:o + MLA_QK_PAD].astype(BF16)

    yield from _rounds([head_stages(h) for h in range(MLA_HEADS)])


def _mixer_in_body(blocks_per_seq, x_ref, g_ref, whg_ref, wmla_ref, wrw_ref,
                   cos_ref, sin_ref, gqa_ref, wqb_ref, gkva_ref, wkvb_ref, gq_ref, gk_ref, lb_ref, hgain_ref,
                   prw_ref, q_ref, k_ref, v_ref, ohg_ref,
                   h_s, phg_s, pmla_s, st_ref):
    @pl.when(pl.program_id(0) % blocks_per_seq == 0)
    def _():
        st_ref[...] = jnp.zeros_like(st_ref)

    h_s[...] = (_rms(x_ref[...], x_ref.shape[-1]) * g_ref[...]).astype(BF16)

    def projections():
        pmla_s[...] = _dot(h_s[...], wmla_ref[...])
        yield
        for c0 in range(0, whg_ref.shape[1], PROJ_COLS):
            phg_s[:, c0:c0 + PROJ_COLS] = _dot(h_s[...], whg_ref[:, c0:c0 + PROJ_COLS])
            yield
        for c0 in range(0, wrw_ref.shape[1], PROJ_COLS):
            c1 = min(c0 + PROJ_COLS, wrw_ref.shape[1])
            prw_ref[:, c0:c1] = _dot(h_s[...], wrw_ref[:, c0:c1])
            yield

    def hgrn2_blocks():
        for first in range(0, x_ref.shape[0] // HG_CHUNK, HG_CHUNKS_PER_STEP):
            yield from _hgrn2_stages(phg_s, lb_ref, hgain_ref, ohg_ref, st_ref, first)

    proj = projections()
    next(proj)
    _alongside(proj, whg_ref.shape[1] // PROJ_COLS,
               _mla_stages(pmla_s, cos_ref, sin_ref, gqa_ref, wqb_ref, gkva_ref, wkvb_ref, gq_ref, gk_ref,
                           q_ref, k_ref, v_ref))
    _alongside(proj, pl.cdiv(wrw_ref.shape[1], PROJ_COLS), hgrn2_blocks())


def mixer_in(x, gain, w_hg, w_mla, w_rw, cos, sin, g_qa, w_qb, g_kva, w_kvb, g_q, g_k, lower_bound, out_gain,
             *, batch, bm):
    t, d = x.shape
    assert t % (batch * bm) == 0 and bm % (HG_CHUNK * HG_CHUNKS_PER_STEP) == 0
    resident = lambda a: pl.BlockSpec(a.shape, lambda i: (0,) * a.ndim, pipeline_mode=pl.Buffered(1))
    rows = lambda width: pl.BlockSpec((bm, width), lambda i: (i, 0))
    heads = lambda width: pl.BlockSpec((MLA_HEADS, bm, width), lambda i: (0, i, 0))
    params = [gain.reshape(1, d), w_hg, w_mla, w_rw]
    mla_params = [g_qa.reshape(1, -1), w_qb, g_kva.reshape(1, -1), w_kvb, g_q, g_k]
    hg_params = [lower_bound.reshape(1, HG_WIDTH), out_gain.reshape(1, HG_HEAD_DIM)]
    return pl.pallas_call(
        functools.partial(_mixer_in_body, t // batch // bm),
        grid=(t // bm,),
        in_specs=([rows(d)] + [resident(a) for a in params] + [rows(LANES), rows(LANES)]
                  + [resident(a) for a in mla_params + hg_params]),
        out_specs=[rows(w_rw.shape[1]), heads(MLA_QK_PAD), heads(MLA_QK_PAD), heads(MLA_V), rows(HG_WIDTH)],
        out_shape=[jax.ShapeDtypeStruct((t, w_rw.shape[1]), F32),
                   jax.ShapeDtypeStruct((MLA_HEADS, t, MLA_QK_PAD), BF16),
                   jax.ShapeDtypeStruct((MLA_HEADS, t, MLA_QK_PAD), BF16),
                   jax.ShapeDtypeStruct((MLA_HEADS, t, MLA_V), BF16),
                   jax.ShapeDtypeStruct((t, HG_WIDTH), BF16)],
        scratch_shapes=[pltpu.VMEM((bm, d), BF16), pltpu.VMEM((bm, w_hg.shape[1]), F32),
                        pltpu.VMEM((bm, w_mla.shape[1]), F32),
                        pltpu.VMEM((HG_HEADS, HG_HEAD_DIM, HG_HEAD_DIM), F32)],
        compiler_params=_cparams(("arbitrary",), 56),
        name="mixer_in",
    )(x, *params, cos, sin, *mla_params, *hg_params)


def _pipelined(stages):
    n = len(stages)
    done = [False] * n
    rnd = 0
    while not all(done):
        for i in range(min(rnd, n - 1) + 1):
            if not done[i]:
                try:
                    next(stages[i])
                except StopIteration:
                    done[i] = True
        rnd += 1


def _flash_body(q_ref, k_ref, v_ref, o_ref):
    i = pl.program_id(2)
    blk = q_ref.shape[1]
    rb = blk // ATTN_ROW_CHAINS

    def unit(state, r, j, diagonal):
        rows = slice(r * rb, (r + 1) * rb)
        ncol = (r + 1) * rb if diagonal else blk
        keys = slice(j * blk, j * blk + ncol)
        s = _dot_nt(q_ref[0, rows, :], k_ref[0, keys, :])
        yield
        if diagonal:
            row = lax.broadcasted_iota(jnp.int32, (rb, rb), 0)
            col = lax.broadcasted_iota(jnp.int32, (rb, rb), 1)
            last = jnp.where(col <= row, s[:, r * rb:], MASK_VALUE)
            s = jnp.concatenate([s[:, :r * rb], last], axis=1) if r else last
        m_old, l_old, acc_old = state[r]
        m_new = jnp.maximum(m_old, jnp.max(s, axis=-1, keepdims=True))
        alpha = jnp.exp(m_old - m_new)
        p = jnp.exp(s - jnp.concatenate([m_new] * (ncol // LANES), axis=1))
        l_new = alpha * l_old + jnp.sum(p, axis=-1, keepdims=True)
        pb = p.astype(BF16)
        yield
        acc = alpha * acc_old + _dot(pb, v_ref[0, keys, :])
        if diagonal:
            o_ref[rows, :] = (acc / l_new).astype(o_ref.dtype)
        else:
            state[r] = (m_new, l_new, acc)

    def query_block(nfull):
        start = (jnp.full((rb, LANES), MASK_VALUE, F32), jnp.zeros((rb, LANES), F32), jnp.zeros((rb, MLA_V), F32))
        state = [start] * ATTN_ROW_CHAINS
        units = [unit(state, r, j, False) for j in range(nfull) for r in range(ATTN_ROW_CHAINS)]
        units += [unit(state, r, nfull, True) for r in range(ATTN_ROW_CHAINS)]
        _pipelined(units)

    for nfull in range(k_ref.shape[1] // blk):
        pl.when(i == nfull)(functools.partial(query_block, nfull))


def flash_attn(q, k, v, *, batch, blk):
    heads, t, _ = q.shape
    seq = t // batch
    nb = seq // blk
    assert t == batch * nb * blk and blk % (ATTN_ROW_CHAINS * LANES) == 0
    kv = lambda width: pl.BlockSpec((1, seq, width), lambda b, h, i: (h, b, 0))
    return pl.pallas_call(
        _flash_body,
        grid=(batch, heads, nb),
        in_specs=[pl.BlockSpec((1, blk, MLA_QK_PAD), lambda b, h, i: (h, b * nb + i, 0)), kv(MLA_QK_PAD), kv(MLA_V)],
        out_specs=pl.BlockSpec((blk, MLA_V), lambda b, h, i: (b * nb + i, h)),
        out_shape=jax.ShapeDtypeStruct((t, heads * MLA_V), BF16),
        compiler_params=_cparams(("parallel", "parallel", "arbitrary"), 48),
        name="flash_attn",
    )(q, k, v)


def _head_sums(x, weight):
    half = 2 * LANES
    gi = lax.broadcasted_iota(jnp.int32, (half, half), 0) // RW_HEAD
    gj = lax.broadcasted_iota(jnp.int32, (half, half), 1) // RW_HEAD
    sel = jnp.where(gi == gj, weight, 0.0).astype(BF16)
    return jnp.concatenate([_dot_sel_rhs(x[:, 0:half], sel, 2), _dot_sel_rhs(x[:, half:], sel, 2)], axis=1)


def _rounds(stages):
    results = [None] * len(stages)
    live = list(range(len(stages)))
    while live:
        for i in list(live):
            try:
                next(stages[i])
            except StopIteration as stop:
                results[i] = stop.value
                live.remove(i)
        yield
    return results


def _lock_step(stages):
    rounds = _rounds(stages)
    while True:
        try:
            next(rounds)
        except StopIteration as stop:
            return stop.value


def _alongside(main, steps, side):
    for _ in range(steps):
        next(main)
        next(side, None)
    for _ in side:
        pass


def _unit_lower_inverse(a, block):
    n = a.shape[0]
    row = lax.broadcasted_iota(jnp.int32, (n, n), 0)
    col = lax.broadcasted_iota(jnp.int32, (n, n), 1)
    diff = row ^ col
    t = jnp.where(row == col, 1.0, 0.0) + jnp.where(diff < 2, a, 0.0)
    size = 2
    while size < block:
        off = jnp.where((diff >= size) & (diff < 2 * size), a, 0.0).astype(BF16)
        tb = t.astype(BF16)
        half = _dot(tb, off).astype(BF16)
        yield
        t = t + _dot(half, tb)
        yield
        size *= 2
    return t


def _rw_chunk_body(has_vres, *refs):
    if has_vres:
        (p_ref, prev_ref, mu_ref, w0_ref, w2_ref, a0_ref, a2_ref, g2_ref, kk_ref, ka_ref, rk_ref,
         v0_ref, v1_ref, v2_ref, vf_ref,
         m_ref, nt_ref, dec_ref, qh_ref, oh_ref, gate_ref, bonus_ref,
         r_s, g_s, k_s, v_s, kk_s, b_s) = refs
    else:
        (p_ref, prev_ref, mu_ref, w0_ref, w2_ref, a0_ref, a2_ref, g2_ref, kk_ref, ka_ref, rk_ref,
         m_ref, nt_ref, dec_ref, qh_ref, oh_ref, gate_ref, bonus_ref, vout_ref,
         r_s, g_s, k_s, v_s, kk_s, b_s) = refs
    c = RW_CHUNK
    w = RW_WIDTH
    bt = p_ref.shape[0]

    p = p_ref[...]
    first = pl.program_id(1) == 0
    prev_row = jnp.where(first, 0.0, prev_ref[7:8, :])
    rowi = lax.broadcasted_iota(jnp.int32, p.shape, 0)
    prev = jnp.where(rowi == 0, prev_row, pltpu.roll(p, 1, 0))
    p = p + (prev - p) * mu_ref[...]
    r = p[:, 0:w]
    k = p[:, w:2 * w]
    v = p[:, 2 * w:3 * w]
    wd = p[:, 3 * w:3 * w + RW_W_RANK]
    ad = p[:, 3 * w + RW_W_RANK:3 * w + RW_W_RANK + RW_A_RANK]
    gd = p[:, 3 * w + RW_W_RANK + RW_A_RANK:]
    w_log = -jax.nn.softplus(-(w0_ref[...] + _dot(jnp.tanh(wd).astype(BF16), w2_ref[...]))) - 0.5
    lr = jax.nn.sigmoid(a0_ref[...] + _dot(ad.astype(BF16), a2_ref[...]))
    gate = _dot(jax.nn.sigmoid(gd).astype(BF16), g2_ref[...])
    if has_vres:
        mix = jax.nn.sigmoid(v0_ref[...] + _dot(_dot(v.astype(BF16), v1_ref[...]).astype(BF16), v2_ref[...]))
        v = v + (vf_ref[...] - v) * mix
    else:
        vout_ref[...] = v
    kk = k * kk_ref[...]
    kk = kk / jnp.maximum(jnp.sqrt(_head_sums(kk * kk, 1.0)), 1e-12)
    k = k * (1.0 + (lr - 1.0) * ka_ref[...])
    gate_ref[...] = gate
    bonus_ref[...] = _head_sums(r * k * rk_ref[...], 1.0) * v
    r_s[...] = r
    g_s[...] = -jnp.exp(w_log)
    k_s[...] = k
    v_s[...] = v
    kk_s[...] = kk
    b_s[...] = kk * lr

    n2 = 2 * c
    row = lax.broadcasted_iota(jnp.int32, (n2, n2), 0)
    col = lax.broadcasted_iota(jnp.int32, (n2, n2), 1)
    same_head = (row ^ col) < c
    incl = same_head & (col <= row)
    strict = same_head & (col < row)
    ci_row = lax.broadcasted_iota(jnp.int32, (c, c), 0)
    ci_col = lax.broadcasted_iota(jnp.int32, (c, c), 1)
    cum = (ci_col <= ci_row).astype(BF16)
    lane = lax.broadcasted_iota(jnp.int32, (1, LANES), 1)
    head0 = lane < RW_HEAD

    def stack(z):
        return jnp.concatenate([jnp.where(head0, z, 0.0), jnp.where(head0, 0.0, z)], axis=0)

    def pair_matrices(g, r, k, v, kkc, bc):
        gc = _dot_sel_lhs(cum, g)
        yield
        g_end = gc[c - 1:c, :]
        inv = jnp.exp(-gc)
        to_end = jnp.exp(g_end - gc)
        a2 = stack(-kkc * jnp.exp(gc - g)).astype(BF16)
        r2 = stack(r * jnp.exp(gc))
        v2 = stack(v).astype(BF16)
        b_end = stack(bc * to_end).astype(BF16)
        k_end = stack(k * to_end).astype(BF16)
        b_inv = (bc * inv).astype(BF16)
        k_inv = (k * inv).astype(BF16)
        ar = jnp.concatenate([a2, r2.astype(BF16)], axis=0)
        abk = _dot_nt(ar, jnp.concatenate([b_inv, b_inv, k_inv, k_inv], axis=0))
        yield
        a_ab = jnp.where(strict, abk[0:n2, 0:n2], 0.0)
        a_rb = jnp.where(incl, abk[n2:2 * n2, 0:n2], 0.0).astype(BF16)
        a_ak = jnp.where(strict, abk[0:n2, n2:2 * n2], 0.0).astype(BF16)
        a_rk = jnp.where(incl, abk[n2:2 * n2, n2:2 * n2], 0.0).astype(BF16)
        x = _dot(a_ak, v2).astype(BF16)
        nt = _dot_tn(v2, k_end)
        oh2 = _dot(a_rk, v2)
        t_inv = (yield from _unit_lower_inverse(a_ab, c)).astype(BF16)
        p12 = _dot(t_inv, jnp.concatenate([a2, x], axis=1)).astype(BF16)
        yield
        qo = _dot(a_rb, p12)
        qh2 = r2 + qo[:, 0:LANES]
        oh2 = oh2 + qo[:, LANES:2 * LANES]
        m = _dot_tn(b_end, p12[:, 0:LANES])
        nt = nt + _dot_tn(p12[:, LANES:2 * LANES], b_end)
        return m, nt, jnp.exp(g_end), qh2[0:c] + qh2[c:n2], oh2[0:c] + oh2[c:n2]

    def chunk(ci, carry):
        units = [(ci * RW_CHUNKS_PER_STEP + u, pr, slice(pr * LANES, (pr + 1) * LANES))
                 for u in range(RW_CHUNKS_PER_STEP) for pr in range(RW_PAIRS)]
        rows = lambda cj: pl.ds(pl.multiple_of(cj * c, c), c)
        loaded = [[s[rows(cj), sl] for s in (g_s, r_s, k_s, v_s, kk_s, b_s)] for cj, _, sl in units]
        results = _lock_step([pair_matrices(*operands) for operands in loaded])
        for (cj, pr, sl), (m, nt, dec, qh, oh) in zip(units, results):
            m_ref[cj, pr] = m.astype(m_ref.dtype)
            nt_ref[cj, pr] = nt
            dec_ref[cj, pr] = dec
            qh_ref[rows(cj), sl] = qh.astype(qh_ref.dtype)
            oh_ref[rows(cj), sl] = oh
        return carry

    lax.fori_loop(0, bt // c // RW_CHUNKS_PER_STEP, chunk, 0)


def rw_chunk(p_rw, mu, w0, w2, a0, a2, g2, k_k, k_a, r_k, vres, *, batch, bt):
    t = p_rw.shape[0]
    nb = t // batch // bt
    nc = bt // RW_CHUNK
    assert t == batch * nb * bt and bt % (RW_CHUNK * RW_CHUNKS_PER_STEP) == 0
    has_vres = vres is not None
    row1 = lambda a: a.reshape(1, -1)
    full = lambda a: pl.BlockSpec(a.shape, lambda b, s: (0,) * a.ndim)
    rows = lambda width: pl.BlockSpec((bt, width), lambda b, s: (b * nb + s, 0))
    prev_spec = pl.BlockSpec((8, RW_IN), lambda b, s: (jnp.maximum((b * nb + s) * (bt // 8) - 1, 0), 0))
    mats = pl.BlockSpec((nc, RW_PAIRS, LANES, LANES), lambda b, s: (b * nb + s, 0, 0, 0))
    params = [row1(mu), row1(w0), w2, row1(a0), a2, g2, row1(k_k), row1(k_a), row1(r_k)]
    args = [p_rw, p_rw] + params
    in_specs = [rows(RW_IN), prev_spec] + [full(a) for a in params]
    if has_vres:
        v0, v1, v2, v_first = vres
        extra = [row1(v0), v1, v2]
        args += extra + [v_first]
        in_specs += [full(a) for a in extra] + [rows(RW_WIDTH)]
    decs = pl.BlockSpec((nc, RW_PAIRS, 1, LANES), lambda b, s: (b * nb + s, 0, 0, 0))
    mat_shape = lambda dt: jax.ShapeDtypeStruct((t // RW_CHUNK, RW_PAIRS, LANES, LANES), dt)
    dec_shape = jax.ShapeDtypeStruct((t // RW_CHUNK, RW_PAIRS, 1, LANES), F32)
    tok_shape = lambda dt: jax.ShapeDtypeStruct((t, RW_WIDTH), dt)
    out_specs = [mats, mats, decs, rows(RW_WIDTH), rows(RW_WIDTH), rows(RW_WIDTH), rows(RW_WIDTH)]
    out_shape = [mat_shape(BF16), mat_shape(F32), dec_shape, tok_shape(BF16), tok_shape(F32), tok_shape(F32),
                 tok_shape(F32)]
    if not has_vres:
        out_specs.append(rows(RW_WIDTH))
        out_shape.append(tok_shape(F32))
    return pl.pallas_call(
        functools.partial(_rw_chunk_body, has_vres),
        grid=(batch, nb),
        in_specs=in_specs,
        out_specs=out_specs,
        out_shape=out_shape,
        scratch_shapes=[pltpu.VMEM((bt, RW_WIDTH), F32) for _ in range(6)],
        compiler_params=_cparams(("parallel", "arbitrary"), 48),
        name="rw_chunk",
    )(*args)


def _rw_scan_stages(m_ref, nt_ref, dec_ref, qh_ref, oh_ref, gate_ref, bonus_ref, lnw_ref, lnb_ref, o_ref, y_ref, h_ref):
    c = RW_CHUNK
    states = [h_ref[pr] for pr in range(RW_PAIRS)]
    for ci in range(qh_ref.shape[0] // c):
        rows = slice(ci * c, (ci + 1) * c)
        for pr in range(RW_PAIRS):
            sl = slice(pr * LANES, (pr + 1) * LANES)
            ht = states[pr]
            hb = ht.astype(BF16)
            y_ref[rows, sl] = _dot_nt(qh_ref[rows, sl], hb) + oh_ref[rows, sl]
            states[pr] = ht * dec_ref[ci, pr] + _dot_nt(hb, m_ref[ci, pr]) + nt_ref[ci, pr]
        yield
    for pr in range(RW_PAIRS):
        h_ref[pr] = states[pr]
    y = y_ref[...]
    mean = _head_sums(y, 1.0 / RW_HEAD)
    d = y - mean
    var = _head_sums(d * d, 1.0 / RW_HEAD)
    y = d * lax.rsqrt(var + RW_GN_EPS) * lnw_ref[...] + lnb_ref[...]
    o_ref[...] = ((y + bonus_ref[...]) * gate_ref[...]).astype(o_ref.dtype)


def _out_mix_body(blocks_per_seq, x_ref, ohg_ref, omla_ref, whg_ref, wmla_ref, wrw_ref,
                  m_ref, nt_ref, dec_ref, qh_ref, oh_ref, gate_ref, bonus_ref, lnw_ref, lnb_ref,
                  o_ref, orw_s, y_s, h_s):
    @pl.when(pl.program_id(0) % blocks_per_seq == 0)
    def _():
        h_s[...] = jnp.zeros_like(h_s)

    d = x_ref.shape[1]

    def projections():
        for c0 in range(0, d, OUT_COLS):
            cols = slice(c0, c0 + OUT_COLS)
            o_ref[:, cols] = (x_ref[:, cols] + _dot(ohg_ref[...], whg_ref[:, cols])
                              + _dot(omla_ref[...], wmla_ref[:, cols]))
            yield

    scan = _rw_scan_stages(m_ref, nt_ref, dec_ref, qh_ref, oh_ref, gate_ref, bonus_ref, lnw_ref, lnb_ref,
                           orw_s, y_s, h_s)
    _alongside(projections(), d // OUT_COLS, scan)
    o_ref[...] += _dot(orw_s[...], wrw_ref[...])


def out_mix(x, o_hg, o_mla, w_hg, w_mla, w_rw, m, nt, dec, qh, oh, gate, bonus, ln_w, ln_b, *, batch, bs):
    t, d = x.shape
    nc = bs // RW_CHUNK
    assert t % (batch * bs) == 0 and d // OUT_COLS >= nc
    row = lambda width: pl.BlockSpec((bs, width), lambda i: (i, 0))
    resident = lambda a: pl.BlockSpec(a.shape, lambda i: (0,) * a.ndim, pipeline_mode=pl.Buffered(1))
    mats = pl.BlockSpec((nc, RW_PAIRS, LANES, LANES), lambda i: (i, 0, 0, 0))
    decs = pl.BlockSpec((nc, RW_PAIRS, 1, LANES), lambda i: (i, 0, 0, 0))
    ln_w, ln_b = ln_w.reshape(1, -1), ln_b.reshape(1, -1)
    return pl.pallas_call(
        functools.partial(_out_mix_body, t // batch // bs),
        grid=(t // bs,),
        in_specs=[row(d), row(HG_WIDTH), row(MLA_WIDTH), resident(w_hg), resident(w_mla), resident(w_rw),
                  mats, mats, decs, row(RW_WIDTH), row(RW_WIDTH), row(RW_WIDTH), row(RW_WIDTH),
                  resident(ln_w), resident(ln_b)],
        out_specs=row(d),
        out_shape=jax.ShapeDtypeStruct((t, d), F32),
        scratch_shapes=[pltpu.VMEM((bs, RW_WIDTH), BF16), pltpu.VMEM((bs, RW_WIDTH), F32),
                        pltpu.VMEM((RW_PAIRS, LANES, LANES), F32)],
        compiler_params=_cparams(("arbitrary",), 56),
        name="out_mix",
    )(x, o_hg, o_mla, w_hg, w_mla, w_rw, m, nt, dec, qh, oh, gate, bonus, ln_w, ln_b)


def _rope_tables(positions):
    half = MLA_ROPE // 2
    inv_freq = ROPE_THETA ** (-jnp.arange(half, dtype=F32) / half)
    ang = positions.astype(F32).reshape(-1, 1) * inv_freq
    zeros = jnp.zeros((ang.shape[0], LANES - MLA_ROPE), F32)
    cos = jnp.concatenate([jnp.cos(ang), jnp.cos(ang), zeros], axis=1)
    sin = jnp.concatenate([jnp.sin(ang), jnp.sin(ang), zeros], axis=1)
    return cos, sin


def _pad_cols(a, width):
    return jnp.pad(a, [(0, 0)] * (a.ndim - 1) + [(0, width - a.shape[-1])])


def _forward(x, positions, attn_norm, w_in, hg_lower_bounds, hg_out_norm,
             mla_q_a_norm, mla_w_qb, mla_kv_a_norm, mla_w_kvb, mla_q_norm, mla_k_norm,
             rw_mu, rw_w0, rw_w2, rw_a0, rw_a2, rw_g2, rw_v0, rw_v1, rw_v2,
             rw_k_k, rw_k_a, rw_r_k, rw_ln_w, rw_ln_b,
             w_o, ffn_norm, w_gate_up, w_down, *, cfg):
    batch, seq, d = x.shape
    depth = w_in.shape[0]
    t = batch * seq
    x = x.reshape(t, d)

    lb_sm = jax.nn.softmax(hg_lower_bounds.astype(F32), axis=0)
    lower_bounds = jnp.cumsum(lb_sm, axis=0) - lb_sm[0]
    cos, sin = _rope_tables(positions)

    hg_end = 4 * HG_WIDTH
    mla_end = hg_end + MLA_IN
    w_in_hg = w_in[:, :, :hg_end].astype(BF16)
    w_in_mla = _pad_cols(w_in[:, :, hg_end:mla_end], MLA_IN_PAD).astype(BF16)
    w_in_rw = w_in[:, :, mla_end:].astype(BF16)
    w_qb = _pad_cols(mla_w_qb.reshape(depth, MLA_Q_RANK, MLA_HEADS, MLA_QK_DIM), MLA_QK_PAD)
    w_qb = w_qb.reshape(depth, MLA_Q_RANK, MLA_HEADS * MLA_QK_PAD).astype(BF16)
    w_kvb = mla_w_kvb.astype(BF16)
    split_gain = lambda g: jnp.stack([g[:, :MLA_NOPE], _pad_cols(g[:, MLA_NOPE:], LANES)], axis=1)
    g_q = split_gain(mla_q_norm)
    g_k = split_gain(mla_k_norm)
    w_o_hg = w_o[:, :HG_WIDTH].astype(BF16)
    w_o_mla = w_o[:, HG_WIDTH:HG_WIDTH + MLA_WIDTH].astype(BF16)
    w_o_rw = w_o[:, HG_WIDTH + MLA_WIDTH:].astype(BF16)
    w_gu = w_gate_up.astype(BF16)
    w_dn = w_down.astype(BF16)
    rw_w2b, rw_a2b, rw_g2b = rw_w2.astype(BF16), rw_a2.astype(BF16), rw_g2.astype(BF16)
    rw_v1b, rw_v2b = rw_v1.astype(BF16), rw_v2.astype(BF16)
    r_k = rw_r_k.reshape(depth, RW_WIDTH)

    v_first = None
    for l in range(depth):
        p_rw, q, k, v, o_hg = mixer_in(x, attn_norm[l], w_in_hg[l], w_in_mla[l], w_in_rw[l], cos, sin,
                                       mla_q_a_norm[l], w_qb[l], mla_kv_a_norm[l], w_kvb[l], g_q[l], g_k[l],
                                       lower_bounds[l], hg_out_norm[l], batch=batch, bm=cfg["proj_bm"])
        o_mla = flash_attn(q, k, v, batch=batch, blk=cfg["attn_blk"])

        vres = None if l == 0 else (rw_v0[l - 1], rw_v1b[l - 1], rw_v2b[l - 1], v_first)
        outs = rw_chunk(p_rw, rw_mu[l], rw_w0[l], rw_w2b[l], rw_a0[l], rw_a2b[l], rw_g2b[l],
                        rw_k_k[l], rw_k_a[l], r_k[l], vres, batch=batch, bt=cfg["rw_bt"])
        if l == 0:
            v_first = outs[7]
        x = out_mix(x, o_hg, o_mla, w_o_hg[l], w_o_mla[l], w_o_rw[l], *outs[:7], rw_ln_w[l], rw_ln_b[l],
                    batch=batch, bs=cfg["out_bm"])
        x = ffn(x, ffn_norm[l], w_gu[l], w_dn[l], bm=cfg["ffn_bm"], bf=cfg["ffn_bf"])
    return x.reshape(batch, seq, d)


_CFG = dict(proj_bm=256, attn_blk=1024, rw_bt=256,
            out_bm=512, ffn_bm=1024, ffn_bf=512)


def kernel(x, positions, attn_norm, w_in, hg_lower_bounds, hg_out_norm, mla_q_a_norm, mla_w_qb, mla_kv_a_norm,
           mla_w_kvb, mla_q_norm, mla_k_norm, rw_mu, rw_w0, rw_w2, rw_a0, rw_a2, rw_g2, rw_v0, rw_v1, rw_v2,
           rw_k_k, rw_k_a, rw_r_k, rw_ln_w, rw_ln_b, w_o, ffn_norm, w_gate_up, w_down):
    return _forward(x, positions, attn_norm, w_in, hg_lower_bounds, hg_out_norm, mla_q_a_norm, mla_w_qb,
                    mla_kv_a_norm, mla_w_kvb, mla_q_norm, mla_k_norm, rw_mu, rw_w0, rw_w2, rw_a0, rw_a2, rw_g2,
                    rw_v0, rw_v1, rw_v2, rw_k_k, rw_k_a, rw_r_k, rw_ln_w, rw_ln_b, w_o, ffn_norm, w_gate_up,
                    w_down, cfg=_CFG)
```

```python
import functools

import jax
import jax.numpy as jnp
from jax import lax
from jax.experimental import pallas as pl
from jax.experimental.pallas import tpu as pltpu

F32 = jnp.float32
BF16 = jnp.bfloat16

HG_WIDTH = 512
HG_HEAD_DIM = 128
HG_HEADS = HG_WIDTH // HG_HEAD_DIM
MLA_NOPE = 128
MLA_ROPE = 64
MLA_V = 128
MLA_HEADS = 8
MLA_WIDTH = MLA_HEADS * MLA_V
MLA_Q_RANK = 512
MLA_KV_RANK = 256
MLA_QK_DIM = MLA_NOPE + MLA_ROPE
MLA_QK_PAD = 256
MLA_IN = MLA_Q_RANK + MLA_KV_RANK + MLA_ROPE
MLA_IN_PAD = 896
ROPE_THETA = 10000.0
MASK_VALUE = -1e30
LOG2_E = 1.4426950408889634
RW_WIDTH = 512
RW_HEAD = 64
RW_HEADS = RW_WIDTH // RW_HEAD
RW_PAIRS = RW_HEADS // 2
RW_W_RANK = 64
RW_A_RANK = 64
RW_G_RANK = 128
RW_IN = 3 * RW_WIDTH + RW_W_RANK + RW_A_RANK + RW_G_RANK
RW_GN_EPS = 64e-5
RMS_EPS = 1e-6

HG_CHUNK = 64
HG_SUB = 8
HG_CHUNKS_PER_STEP = 4
RW_CHUNK = 64
RW_CHUNKS_PER_STEP = 4
ATTN_ROW_CHAINS = 4
PROJ_COLS = 512
OUT_COLS = 256
LANES = 128

_MIB = 1024 * 1024


def _cparams(semantics, vmem_mib):
    return pltpu.CompilerParams(dimension_semantics=semantics, vmem_limit_bytes=vmem_mib * _MIB)


def _dot(a, b):
    return jnp.dot(a, b, preferred_element_type=F32)


def _dot_nt(a, b):
    return lax.dot_general(a, b, (((1,), (1,)), ((), ())), preferred_element_type=F32)


def _dot_tn(a, b):
    return lax.dot_general(a, b, (((0,), (0,)), ((), ())), preferred_element_type=F32)


def _split(x, terms):
    parts = []
    for _ in range(terms - 1):
        hi = x.astype(BF16)
        parts.append(hi)
        x = x - hi.astype(F32)
    parts.append(x.astype(BF16))
    return parts


def _dot_sel_lhs(sel, x, terms=3):
    return sum(_dot(sel, part) for part in _split(x, terms))


def _dot_sel_rhs(x, sel, terms=3):
    return sum(_dot(part, sel) for part in _split(x, terms))


def _rms(x, width):
    ms = jnp.sum(x * x, axis=-1, keepdims=True) * (1.0 / width)
    return x * lax.rsqrt(ms + RMS_EPS)


def _ffn_body(x_ref, g_ref, wg_ref, wu_ref, wd_ref, o_ref, h_ref):
    @pl.when(pl.program_id(1) == 0)
    def _():
        x = x_ref[...]
        h_ref[...] = (_rms(x, x.shape[-1]) * g_ref[...]).astype(BF16)
        o_ref[...] = x

    h = h_ref[...]
    gate = _dot(h, wg_ref[...])
    up = _dot(h, wu_ref[...])
    act = (gate * jax.nn.sigmoid(gate) * up).astype(BF16)
    o_ref[...] += _dot(act, wd_ref[...])


def ffn(x, gain, w_gate_up, w_down, *, bm, bf):
    t, d = x.shape
    dff = w_down.shape[0]
    nf = dff // bf
    return pl.pallas_call(
        _ffn_body,
        grid=(t // bm, nf),
        in_specs=[
            pl.BlockSpec((bm, d), lambda i, j: (i, 0)),
            pl.BlockSpec((1, d), lambda i, j: (0, 0)),
            pl.BlockSpec((d, bf), lambda i, j: (0, j)),
            pl.BlockSpec((d, bf), lambda i, j: (0, j + nf)),
            pl.BlockSpec((bf, d), lambda i, j: (j, 0)),
        ],
        out_specs=pl.BlockSpec((bm, d), lambda i, j: (i, 0)),
        out_shape=jax.ShapeDtypeStruct((t, d), F32),
        scratch_shapes=[pltpu.VMEM((bm, d), BF16)],
        compiler_params=_cparams(("parallel", "arbitrary"), 56),
        name="ffn",
    )(x, gain.reshape(1, d), w_gate_up, w_gate_up, w_down)


def _hgrn2_scores(q, k, b):
    c, sub = HG_CHUNK, HG_SUB
    nsub = c // sub
    row = lax.broadcasted_iota(jnp.int32, (c, c), 0)
    col = lax.broadcasted_iota(jnp.int32, (c, c), 1)
    blk_t, blk_s = row // sub, col // sub
    zero = jnp.zeros_like(b[0:1, :])
    s = jnp.zeros((c, c), F32)
    g = sub
    while g < c:
        ends = [b[g * i + g - 1:g * i + g, :] for i in range(c // g)]
        per_group = lambda vals: jnp.concatenate([jnp.broadcast_to(x, (g, b.shape[1])) for x in vals], axis=0)
        q_up = (q * jnp.exp(b - per_group([zero] + ends[:-1]))).astype(BF16)
        k_low = (k * jnp.exp(per_group(ends) - b)).astype(BF16)
        halves = (row // g == col // g + 1) & (row // (2 * g) == col // (2 * g))
        s = s + jnp.where(halves, _dot_nt(q_up, k_low), 0.0)
        g *= 2
    lane = lax.broadcasted_iota(jnp.int32, (sub, c), 1)
    tiles = []
    for i in range(nsub):
        q_blk, b_blk = q[sub * i:sub * (i + 1), :], b[sub * i:sub * (i + 1), :]
        tile = jnp.zeros((sub, c), F32)
        for key in range(sub * i, sub * (i + 1)):
            decay = jnp.exp(jnp.minimum(b_blk - b[key:key + 1, :], 0.0))
            column = jnp.sum(q_blk * decay * k[key:key + 1, :], axis=-1, keepdims=True)
            tile = jnp.where(lane == key, column, tile)
        tiles.append(tile)
    return s + jnp.where((blk_t == blk_s) & (col <= row), jnp.concatenate(tiles, axis=0), 0.0)


def _hgrn2_stages(p_ref, lb_ref, gain_ref, o_ref, st_ref, first_chunk):
    c = HG_CHUNK
    w = HG_WIDTH
    row = lax.broadcasted_iota(jnp.int32, (c, c), 0)
    col = lax.broadcasted_iota(jnp.int32, (c, c), 1)
    tril = col <= row
    cum = tril.astype(BF16)
    lb = lb_ref[...]
    gain = gain_ref[...]
    heads = [slice(h * HG_HEAD_DIM, (h + 1) * HG_HEAD_DIM) for h in range(HG_HEADS)]
    states = [[st_ref[h] for h in range(HG_HEADS)]] + [None] * HG_CHUNKS_PER_STEP

    def chunk_stages(rows, u):
        q_raw = p_ref[rows, 0:w]
        z = p_ref[rows, w:2 * w]
        v = p_ref[rows, 2 * w:3 * w]
        g_raw = p_ref[rows, 3 * w:4 * w]
        q = q_raw * jax.nn.sigmoid(q_raw)
        log_f = jnp.log(lb + (1.0 - lb) * jax.nn.sigmoid(z))
        k = (1.0 - lb) * jax.nn.sigmoid(-z)
        b = _dot_sel_lhs(cum, log_f)
        yield
        b_end = b[c - 1:c, :]
        q_st = (q * jnp.exp(b)).astype(BF16)
        k_st = (k * jnp.exp(b_end - b)).astype(BF16)
        d_end = jnp.exp(b_end)
        vb = v.astype(BF16)
        gate = g_raw * jax.nn.sigmoid(g_raw)
        scores = [_hgrn2_scores(q[:, sl], k[:, sl], b[:, sl]).astype(BF16) for sl in heads]
        updates = [_dot_tn(vb[:, sl], k_st[:, sl]) for sl in heads]
        yield
        st = states[u]
        states[u + 1] = [st[h] * d_end[:, sl] + updates[h] for h, sl in enumerate(heads)]
        outs = [_dot_nt(q_st[:, sl], st[h].astype(BF16)) + _dot(scores[h], vb[:, sl]) for h, sl in enumerate(heads)]
        yield
        for h, sl in enumerate(heads):
            o_ref[rows, sl] = (_rms(outs[h], HG_HEAD_DIM) * gain * gate[:, sl]).astype(o_ref.dtype)

    yield from _rounds([chunk_stages(pl.ds(pl.multiple_of((first_chunk + u) * c, c), c), u)
                        for u in range(HG_CHUNKS_PER_STEP)])
    for h in range(HG_HEADS):
        st_ref[h] = states[HG_CHUNKS_PER_STEP][h]


def _rope(x, cos, sin):
    half = MLA_ROPE // 2
    lane = lax.broadcasted_iota(jnp.int32, x.shape, 1)
    rot = jnp.where(lane < half, -pltpu.roll(x, LANES - half, 1), pltpu.roll(x, half, 1))
    return x * cos + rot * sin


def _mla_stages(p_ref, cos_ref, sin_ref, gqa_ref, wqb_ref, gkva_ref, wkvb_ref, gq_ref, gk_ref, q_ref, k_ref, v_ref):
    p = p_ref[...]
    cos = cos_ref[...]
    sin = sin_ref[...]
    scale = MLA_QK_DIM ** -0.5 * LOG2_E
    qn = (_rms(p[:, 0:MLA_Q_RANK], MLA_Q_RANK) * gqa_ref[...]).astype(BF16)
    q = _dot(qn, wqb_ref[...])
    kvn = (_rms(p[:, MLA_Q_RANK:MLA_Q_RANK + MLA_KV_RANK], MLA_KV_RANK) * gkva_ref[...]).astype(BF16)
    kv = _dot(kvn, wkvb_ref[...])
    gq_nope, gq_rope = gq_ref[0:1, :], gq_ref[1:2, :]
    gk_nope, gk_rope = gk_ref[0:1, :], gk_ref[1:2, :]
    k_rope = _rope(_rms(p[:, MLA_Q_RANK + MLA_KV_RANK:], MLA_ROPE) * gk_rope, cos, sin).astype(BF16)
    half = MLA_ROPE // 2
    first_half = lax.broadcasted_iota(jnp.int32, cos.shape, 1) < half
    yield

    def head_stages(h):
        o = h * MLA_QK_PAD
        q_nope, q_rope, k_nope = q[:, o:o + MLA_NOPE], q[:, o + MLA_NOPE:o + MLA_QK_PAD], kv[:, o:o + MLA_NOPE]
        ss_qn = jnp.sum(q_nope * q_nope, axis=-1, keepdims=True)
        ss_qr = jnp.sum(q_rope * q_rope, axis=-1, keepdims=True)
        ss_kn = jnp.sum(k_nope * k_nope, axis=-1, keepdims=True)
        yield
        q_nope = q_nope * lax.rsqrt(ss_qn * (1.0 / MLA_NOPE) + RMS_EPS) * (gq_nope * scale)
        q_rope = q_rope * lax.rsqrt(ss_qr * (1.0 / MLA_ROPE) + RMS_EPS) * (gq_rope * scale)
        k_nope = k_nope * lax.rsqrt(ss_kn * (1.0 / MLA_NOPE) + RMS_EPS) * gk_nope
        up, down = pltpu.roll(q_rope, LANES - half, 1), pltpu.roll(q_rope, half, 1)
        yield
        q_ref[h, :, 0:MLA_NOPE] = q_nope.astype(BF16)
        q_ref[h, :, MLA_NOPE:MLA_QK_PAD] = (q_rope * cos + jnp.where(first_half, -up, down) * sin).astype(BF16)
        k_ref[h, :, 0:MLA_NOPE] = k_nope.astype(BF16)
        k_ref[h, :, MLA_NOPE:MLA_QK_PAD] = k_rope
        v_ref[h] = kv[:, o + MLA_NOPE:o + MLA_QK_PAD].astype(BF16)

    yield from _rounds([head_stages(h) for h in range(MLA_HEADS)])


def _mixer_in_body(blocks_per_seq, x_ref, g_ref, whg_ref, wmla_ref, wrw_ref,
                   cos_ref, sin_ref, gqa_ref, wqb_ref, gkva_ref, wkvb_ref, gq_ref, gk_ref, lb_ref, hgain_ref,
                   prw_ref, q_ref, k_ref, v_ref, ohg_ref,
                   h_s, phg_s, pmla_s, st_ref):
    @pl.when(pl.program_id(0) % blocks_per_seq == 0)
    def _():
        st_ref[...] = jnp.zeros_like(st_ref)

    h_s[...] = (_rms(x_ref[...], x_ref.shape[-1]) * g_ref[...]).astype(BF16)

    def projections():
        pmla_s[...] = _dot(h_s[...], wmla_ref[...])
        yield
        for c0 in range(0, whg_ref.shape[1], PROJ_COLS):
            phg_s[:, c0:c0 + PROJ_COLS] = _dot(h_s[...], whg_ref[:, c0:c0 + PROJ_COLS])
            yield
        for c0 in range(0, wrw_ref.shape[1], PROJ_COLS):
            c1 = min(c0 + PROJ_COLS, wrw_ref.shape[1])
            prw_ref[:, c0:c1] = _dot(h_s[...], wrw_ref[:, c0:c1])
            yield

    def hgrn2_blocks():
        for first in range(0, x_ref.shape[0] // HG_CHUNK, HG_CHUNKS_PER_STEP):
            yield from _hgrn2_stages(phg_s, lb_ref, hgain_ref, ohg_ref, st_ref, first)

    proj = projections()
    next(proj)
    _alongside(proj, whg_ref.shape[1] // PROJ_COLS,
               _mla_stages(pmla_s, cos_ref, sin_ref, gqa_ref, wqb_ref, gkva_ref, wkvb_ref, gq_ref, gk_ref,
                           q_ref, k_ref, v_ref))
    _alongside(proj, pl.cdiv(wrw_ref.shape[1], PROJ_COLS), hgrn2_blocks())


def mixer_in(x, gain, w_hg, w_mla, w_rw, cos, sin, g_qa, w_qb, g_kva, w_kvb, g_q, g_k, lower_bound, out_gain,
             *, batch, bm):
    t, d = x.shape
    assert t % (batch * bm) == 0 and bm % (HG_CHUNK * HG_CHUNKS_PER_STEP) == 0
    resident = lambda a: pl.BlockSpec(a.shape, lambda i: (0,) * a.ndim, pipeline_mode=pl.Buffered(1))
    rows = lambda width: pl.BlockSpec((bm, width), lambda i: (i, 0))
    heads = lambda width: pl.BlockSpec((MLA_HEADS, bm, width), lambda i: (0, i, 0))
    params = [gain.reshape(1, d), w_hg, w_mla, w_rw]
    mla_params = [g_qa.reshape(1, -1), w_qb, g_kva.reshape(1, -1), w_kvb, g_q, g_k]
    hg_params = [lower_bound.reshape(1, HG_WIDTH), out_gain.reshape(1, HG_HEAD_DIM)]
    return pl.pallas_call(
        functools.partial(_mixer_in_body, t // batch // bm),
        grid=(t // bm,),
        in_specs=([rows(d)] + [resident(a) for a in params] + [rows(LANES), rows(LANES)]
                  + [resident(a) for a in mla_params + hg_params]),
        out_specs=[rows(w_rw.shape[1]), heads(MLA_QK_PAD), heads(MLA_QK_PAD), heads(MLA_V), rows(HG_WIDTH)],
        out_shape=[jax.ShapeDtypeStruct((t, w_rw.shape[1]), F32),
                   jax.ShapeDtypeStruct((MLA_HEADS, t, MLA_QK_PAD), BF16),
                   jax.ShapeDtypeStruct((MLA_HEADS, t, MLA_QK_PAD), BF16),
                   jax.ShapeDtypeStruct((MLA_HEADS, t, MLA_V), BF16),
                   jax.ShapeDtypeStruct((t, HG_WIDTH), BF16)],
        scratch_shapes=[pltpu.VMEM((bm, d), BF16), pltpu.VMEM((bm, w_hg.shape[1]), F32),
                        pltpu.VMEM((bm, w_mla.shape[1]), F32),
                        pltpu.VMEM((HG_HEADS, HG_HEAD_DIM, HG_HEAD_DIM), F32)],
        compiler_params=_cparams(("arbitrary",), 56),
        name="mixer_in",
    )(x, *params, cos, sin, *mla_params, *hg_params)


def _pipelined(stages):
    n = len(stages)
    done = [False] * n
    rnd = 0
    while not all(done):
        for i in range(min(rnd, n - 1) + 1):
            if not done[i]:
                try:
                    next(stages[i])
                except StopIteration:
                    done[i] = True
        rnd += 1


def _flash_body(q_ref, k_ref, v_ref, o_ref):
    i = pl.program_id(2)
    blk = q_ref.shape[1]
    rb = blk // ATTN_ROW_CHAINS

    def unit(state, r, j, diagonal):
        rows = slice(r * rb, (r + 1) * rb)
        ncol = (r + 1) * rb if diagonal else blk
        keys = slice(j * blk, j * blk + ncol)
        s = _dot_nt(q_ref[0, rows, :], k_ref[0, keys, :])
        yield
        if diagonal:
            row = lax.broadcasted_iota(jnp.int32, (rb, rb), 0)
            col = lax.broadcasted_iota(jnp.int32, (rb, rb), 1)
            last = jnp.where(col <= row, s[:, r * rb:], MASK_VALUE)
            s = jnp.concatenate([s[:, :r * rb], last], axis=1) if r else last
        m_old, l_old, acc_old = state[r]
        m_new = jnp.maximum(m_old, jnp.max(s, axis=-1, keepdims=True))
        alpha = jnp.exp2(m_old - m_new)
        p = jnp.exp2(s - jnp.concatenate([m_new] * (ncol // LANES), axis=1))
        l_new = alpha * l_old + jnp.sum(p, axis=-1, keepdims=True)
        pb = p.astype(BF16)
        yield
        acc = alpha * acc_old + _dot(pb, v_ref[0, keys, :])
        if diagonal:
            o_ref[rows, :] = (acc / l_new).astype(o_ref.dtype)
        else:
            state[r] = (m_new, l_new, acc)

    def query_block(nfull):
        start = (jnp.full((rb, LANES), MASK_VALUE, F32), jnp.zeros((rb, LANES), F32), jnp.zeros((rb, MLA_V), F32))
        state = [start] * ATTN_ROW_CHAINS
        units = [unit(state, r, j, False) for j in range(nfull) for r in range(ATTN_ROW_CHAINS)]
        units += [unit(state, r, nfull, True) for r in range(ATTN_ROW_CHAINS)]
        _pipelined(units)

    for nfull in range(k_ref.shape[1] // blk):
        pl.when(i == nfull)(functools.partial(query_block, nfull))


def flash_attn(q, k, v, *, batch, blk):
    heads, t, _ = q.shape
    seq = t // batch
    nb = seq // blk
    assert t == batch * nb * blk and blk % (ATTN_ROW_CHAINS * LANES) == 0
    kv = lambda width: pl.BlockSpec((1, seq, width), lambda b, h, i: (h, b, 0))
    return pl.pallas_call(
        _flash_body,
        grid=(batch, heads, nb),
        in_specs=[pl.BlockSpec((1, blk, MLA_QK_PAD), lambda b, h, i: (h, b * nb + i, 0)), kv(MLA_QK_PAD), kv(MLA_V)],
        out_specs=pl.BlockSpec((blk, MLA_V), lambda b, h, i: (b * nb + i, h)),
        out_shape=jax.ShapeDtypeStruct((t, heads * MLA_V), BF16),
        compiler_params=_cparams(("parallel", "parallel", "arbitrary"), 48),
        name="flash_attn",
    )(q, k, v)


def _head_sums(x, weight):
    half = 2 * LANES
    gi = lax.broadcasted_iota(jnp.int32, (half, half), 0) // RW_HEAD
    gj = lax.broadcasted_iota(jnp.int32, (half, half), 1) // RW_HEAD
    sel = jnp.where(gi == gj, weight, 0.0).astype(BF16)
    return jnp.concatenate([_dot_sel_rhs(x[:, 0:half], sel, 2), _dot_sel_rhs(x[:, half:], sel, 2)], axis=1)


def _rounds(stages):
    results = [None] * len(stages)
    live = list(range(len(stages)))
    while live:
        for i in list(live):
            try:
                next(stages[i])
            except StopIteration as stop:
                results[i] = stop.value
                live.remove(i)
        yield
    return results


def _lock_step(stages):
    rounds = _rounds(stages)
    while True:
        try:
            next(rounds)
        except StopIteration as stop:
            return stop.value


def _alongside(main, steps, side):
    for _ in range(steps):
        next(main)
        next(side, None)
    for _ in side:
        pass


def _unit_lower_inverse(a, block):
    n = a.shape[0]
    row = lax.broadcasted_iota(jnp.int32, (n, n), 0)
    col = lax.broadcasted_iota(jnp.int32, (n, n), 1)
    diff = row ^ col
    t = jnp.where(row == col, 1.0, 0.0) + jnp.where(diff < 2, a, 0.0)
    size = 2
    while size < block:
        off = jnp.where((diff >= size) & (diff < 2 * size), a, 0.0).astype(BF16)
        tb = t.astype(BF16)
        half = _dot(tb, off).astype(BF16)
        yield
        t = t + _dot(half, tb)
        yield
        size *= 2
    return t


def _rw_chunk_body(has_vres, *refs):
    if has_vres:
        (p_ref, prev_ref, mu_ref, w0_ref, w2_ref, a0_ref, a2_ref, g2_ref, kk_ref, ka_ref, rk_ref,
         v0_ref, v1_ref, v2_ref, vf_ref,
         m_ref, nt_ref, dec_ref, qh_ref, oh_ref, gate_ref, bonus_ref,
         r_s, g_s, k_s, v_s, kk_s, b_s) = refs
    else:
        (p_ref, prev_ref, mu_ref, w0_ref, w2_ref, a0_ref, a2_ref, g2_ref, kk_ref, ka_ref, rk_ref,
         m_ref, nt_ref, dec_ref, qh_ref, oh_ref, gate_ref, bonus_ref, vout_ref,
         r_s, g_s, k_s, v_s, kk_s, b_s) = refs
    c = RW_CHUNK
    w = RW_WIDTH
    bt = p_ref.shape[0]

    p = p_ref[...]
    first = pl.program_id(1) == 0
    prev_row = jnp.where(first, 0.0, prev_ref[7:8, :])
    rowi = lax.broadcasted_iota(jnp.int32, p.shape, 0)
    prev = jnp.where(rowi == 0, prev_row, pltpu.roll(p, 1, 0))
    p = p + (prev - p) * mu_ref[...]
    r = p[:, 0:w]
    k = p[:, w:2 * w]
    v = p[:, 2 * w:3 * w]
    wd = p[:, 3 * w:3 * w + RW_W_RANK]
    ad = p[:, 3 * w + RW_W_RANK:3 * w + RW_W_RANK + RW_A_RANK]
    gd = p[:, 3 * w + RW_W_RANK + RW_A_RANK:]
    w_log = -jax.nn.softplus(-(w0_ref[...] + _dot(jnp.tanh(wd).astype(BF16), w2_ref[...]))) - 0.5
    lr = jax.nn.sigmoid(a0_ref[...] + _dot(ad.astype(BF16), a2_ref[...]))
    gate = _dot(jax.nn.sigmoid(gd).astype(BF16), g2_ref[...])
    if has_vres:
        mix = jax.nn.sigmoid(v0_ref[...] + _dot(_dot(v.astype(BF16), v1_ref[...]).astype(BF16), v2_ref[...]))
        v = v + (vf_ref[...] - v) * mix
    else:
        vout_ref[...] = v
    kk = k * kk_ref[...]
    kk = kk / jnp.maximum(jnp.sqrt(_head_sums(kk * kk, 1.0)), 1e-12)
    k = k * (1.0 + (lr - 1.0) * ka_ref[...])
    gate_ref[...] = gate
    bonus_ref[...] = _head_sums(r * k * rk_ref[...], 1.0) * v
    r_s[...] = r
    g_s[...] = -jnp.exp(w_log)
    k_s[...] = k
    v_s[...] = v
    kk_s[...] = kk
    b_s[...] = kk * lr

    n2 = 2 * c
    row = lax.broadcasted_iota(jnp.int32, (n2, n2), 0)
    col = lax.broadcasted_iota(jnp.int32, (n2, n2), 1)
    same_head = (row ^ col) < c
    incl = same_head & (col <= row)
    strict = same_head & (col < row)
    ci_row = lax.broadcasted_iota(jnp.int32, (c, c), 0)
    ci_col = lax.broadcasted_iota(jnp.int32, (c, c), 1)
    cum = (ci_col <= ci_row).astype(BF16)
    lane = lax.broadcasted_iota(jnp.int32, (1, LANES), 1)
    head0 = lane < RW_HEAD

    def stack(z):
        return jnp.concatenate([jnp.where(head0, z, 0.0), jnp.where(head0, 0.0, z)], axis=0)

    def pair_matrices(g, r, k, v, kkc, bc):
        gc = _dot_sel_lhs(cum, g)
        yield
        g_end = gc[c - 1:c, :]
        inv = jnp.exp(-gc)
        to_end = jnp.exp(g_end - gc)
        a2 = stack(-kkc * jnp.exp(gc - g)).astype(BF16)
        r2 = stack(r * jnp.exp(gc))
        v2 = stack(v).astype(BF16)
        b_end = stack(bc * to_end).astype(BF16)
        k_end = stack(k * to_end).astype(BF16)
        b_inv = (bc * inv).astype(BF16)
        k_inv = (k * inv).astype(BF16)
        ar = jnp.concatenate([a2, r2.astype(BF16)], axis=0)
        abk = _dot_nt(ar, jnp.concatenate([b_inv, b_inv, k_inv, k_inv], axis=0))
        yield
        a_ab = jnp.where(strict, abk[0:n2, 0:n2], 0.0)
        a_rb = jnp.where(incl, abk[n2:2 * n2, 0:n2], 0.0).astype(BF16)
        a_ak = jnp.where(strict, abk[0:n2, n2:2 * n2], 0.0).astype(BF16)
        a_rk = jnp.where(incl, abk[n2:2 * n2, n2:2 * n2], 0.0).astype(BF16)
        x = _dot(a_ak, v2).astype(BF16)
        nt = _dot_tn(v2, k_end)
        oh2 = _dot(a_rk, v2)
        t_inv = (yield from _unit_lower_inverse(a_ab, c)).astype(BF16)
        p12 = _dot(t_inv, jnp.concatenate([a2, x], axis=1)).astype(BF16)
        yield
        qo = _dot(a_rb, p12)
        qh2 = r2 + qo[:, 0:LANES]
        oh2 = oh2 + qo[:, LANES:2 * LANES]
        m = _dot_tn(b_end, p12[:, 0:LANES])
        nt = nt + _dot_tn(p12[:, LANES:2 * LANES], b_end)
        return m, nt, jnp.exp(g_end), qh2[0:c] + qh2[c:n2], oh2[0:c] + oh2[c:n2]

    def chunk(ci, carry):
        units = [(ci * RW_CHUNKS_PER_STEP + u, pr, slice(pr * LANES, (pr + 1) * LANES))
                 for u in range(RW_CHUNKS_PER_STEP) for pr in range(RW_PAIRS)]
        rows = lambda cj: pl.ds(pl.multiple_of(cj * c, c), c)
        loaded = [[s[rows(cj), sl] for s in (g_s, r_s, k_s, v_s, kk_s, b_s)] for cj, _, sl in units]
        results = _lock_step([pair_matrices(*operands) for operands in loaded])
        for (cj, pr, sl), (m, nt, dec, qh, oh) in zip(units, results):
            m_ref[cj, pr] = m.astype(m_ref.dtype)
            nt_ref[cj, pr] = nt
            dec_ref[cj, pr] = dec
            qh_ref[rows(cj), sl] = qh.astype(qh_ref.dtype)
            oh_ref[rows(cj), sl] = oh
        return carry

    lax.fori_loop(0, bt // c // RW_CHUNKS_PER_STEP, chunk, 0)


def rw_chunk(p_rw, mu, w0, w2, a0, a2, g2, k_k, k_a, r_k, vres, *, batch, bt):
    t = p_rw.shape[0]
    nb = t // batch // bt
    nc = bt // RW_CHUNK
    assert t == batch * nb * bt and bt % (RW_CHUNK * RW_CHUNKS_PER_STEP) == 0
    has_vres = vres is not None
    row1 = lambda a: a.reshape(1, -1)
    full = lambda a: pl.BlockSpec(a.shape, lambda b, s: (0,) * a.ndim)
    rows = lambda width: pl.BlockSpec((bt, width), lambda b, s: (b * nb + s, 0))
    prev_spec = pl.BlockSpec((8, RW_IN), lambda b, s: (jnp.maximum((b * nb + s) * (bt // 8) - 1, 0), 0))
    mats = pl.BlockSpec((nc, RW_PAIRS, LANES, LANES), lambda b, s: (b * nb + s, 0, 0, 0))
    params = [row1(mu), row1(w0), w2, row1(a0), a2, g2, row1(k_k), row1(k_a), row1(r_k)]
    args = [p_rw, p_rw] + params
    in_specs = [rows(RW_IN), prev_spec] + [full(a) for a in params]
    if has_vres:
        v0, v1, v2, v_first = vres
        extra = [row1(v0), v1, v2]
        args += extra + [v_first]
        in_specs += [full(a) for a in extra] + [rows(RW_WIDTH)]
    decs = pl.BlockSpec((nc, RW_PAIRS, 1, LANES), lambda b, s: (b * nb + s, 0, 0, 0))
    mat_shape = lambda dt: jax.ShapeDtypeStruct((t // RW_CHUNK, RW_PAIRS, LANES, LANES), dt)
    dec_shape = jax.ShapeDtypeStruct((t // RW_CHUNK, RW_PAIRS, 1, LANES), F32)
    tok_shape = lambda dt: jax.ShapeDtypeStruct((t, RW_WIDTH), dt)
    out_specs = [mats, mats, decs, rows(RW_WIDTH), rows(RW_WIDTH), rows(RW_WIDTH), rows(RW_WIDTH)]
    out_shape = [mat_shape(BF16), mat_shape(F32), dec_shape, tok_shape(BF16), tok_shape(F32), tok_shape(F32),
                 tok_shape(F32)]
    if not has_vres:
        out_specs.append(rows(RW_WIDTH))
        out_shape.append(tok_shape(F32))
    return pl.pallas_call(
        functools.partial(_rw_chunk_body, has_vres),
        grid=(batch, nb),
        in_specs=in_specs,
        out_specs=out_specs,
        out_shape=out_shape,
        scratch_shapes=[pltpu.VMEM((bt, RW_WIDTH), F32) for _ in range(6)],
        compiler_params=_cparams(("parallel", "arbitrary"), 48),
        name="rw_chunk",
    )(*args)


def _rw_scan_stages(m_ref, nt_ref, dec_ref, qh_ref, oh_ref, gate_ref, bonus_ref, lnw_ref, lnb_ref, o_ref, y_ref, h_ref):
    c = RW_CHUNK
    states = [h_ref[pr] for pr in range(RW_PAIRS)]
    for ci in range(qh_ref.shape[0] // c):
        rows = slice(ci * c, (ci + 1) * c)
        for pr in range(RW_PAIRS):
            sl = slice(pr * LANES, (pr + 1) * LANES)
            ht = states[pr]
            hb = ht.astype(BF16)
            y_ref[rows, sl] = _dot_nt(qh_ref[rows, sl], hb) + oh_ref[rows, sl]
            states[pr] = ht * dec_ref[ci, pr] + _dot_nt(hb, m_ref[ci, pr]) + nt_ref[ci, pr]
        yield
    for pr in range(RW_PAIRS):
        h_ref[pr] = states[pr]
    y = y_ref[...]
    mean = _head_sums(y, 1.0 / RW_HEAD)
    d = y - mean
    var = _head_sums(d * d, 1.0 / RW_HEAD)
    y = d * lax.rsqrt(var + RW_GN_EPS) * lnw_ref[...] + lnb_ref[...]
    o_ref[...] = ((y + bonus_ref[...]) * gate_ref[...]).astype(o_ref.dtype)


def _out_mix_body(blocks_per_seq, x_ref, ohg_ref, omla_ref, whg_ref, wmla_ref, wrw_ref,
                  m_ref, nt_ref, dec_ref, qh_ref, oh_ref, gate_ref, bonus_ref, lnw_ref, lnb_ref,
                  o_ref, orw_s, y_s, h_s):
    @pl.when(pl.program_id(0) % blocks_per_seq == 0)
    def _():
        h_s[...] = jnp.zeros_like(h_s)

    d = x_ref.shape[1]

    def projections():
        for c0 in range(0, d, OUT_COLS):
            cols = slice(c0, c0 + OUT_COLS)
            o_ref[:, cols] = (x_ref[:, cols] + _dot(ohg_ref[...], whg_ref[:, cols])
                              + _dot(omla_ref[...], wmla_ref[:, cols]))
            yield

    scan = _rw_scan_stages(m_ref, nt_ref, dec_ref, qh_ref, oh_ref, gate_ref, bonus_ref, lnw_ref, lnb_ref,
                           orw_s, y_s, h_s)
    _alongside(projections(), d // OUT_COLS, scan)
    o_ref[...] += _dot(orw_s[...], wrw_ref[...])


def out_mix(x, o_hg, o_mla, w_hg, w_mla, w_rw, m, nt, dec, qh, oh, gate, bonus, ln_w, ln_b, *, batch, bs):
    t, d = x.shape
    nc = bs // RW_CHUNK
    assert t % (batch * bs) == 0 and d // OUT_COLS >= nc
    row = lambda width: pl.BlockSpec((bs, width), lambda i: (i, 0))
    resident = lambda a: pl.BlockSpec(a.shape, lambda i: (0,) * a.ndim, pipeline_mode=pl.Buffered(1))
    mats = pl.BlockSpec((nc, RW_PAIRS, LANES, LANES), lambda i: (i, 0, 0, 0))
    decs = pl.BlockSpec((nc, RW_PAIRS, 1, LANES), lambda i: (i, 0, 0, 0))
    ln_w, ln_b = ln_w.reshape(1, -1), ln_b.reshape(1, -1)
    return pl.pallas_call(
        functools.partial(_out_mix_body, t // batch // bs),
        grid=(t // bs,),
        in_specs=[row(d), row(HG_WIDTH), row(MLA_WIDTH), resident(w_hg), resident(w_mla), resident(w_rw),
                  mats, mats, decs, row(RW_WIDTH), row(RW_WIDTH), row(RW_WIDTH), row(RW_WIDTH),
                  resident(ln_w), resident(ln_b)],
        out_specs=row(d),
        out_shape=jax.ShapeDtypeStruct((t, d), F32),
        scratch_shapes=[pltpu.VMEM((bs, RW_WIDTH), BF16), pltpu.VMEM((bs, RW_WIDTH), F32),
                        pltpu.VMEM((RW_PAIRS, LANES, LANES), F32)],
        compiler_params=_cparams(("arbitrary",), 56),
        name="out_mix",
    )(x, o_hg, o_mla, w_hg, w_mla, w_rw, m, nt, dec, qh, oh, gate, bonus, ln_w, ln_b)


def _rope_tables(positions):
    half = MLA_ROPE // 2
    inv_freq = ROPE_THETA ** (-jnp.arange(half, dtype=F32) / half)
    ang = positions.astype(F32).reshape(-1, 1) * inv_freq
    zeros = jnp.zeros((ang.shape[0], LANES - MLA_ROPE), F32)
    cos = jnp.concatenate([jnp.cos(ang), jnp.cos(ang), zeros], axis=1)
    sin = jnp.concatenate([jnp.sin(ang), jnp.sin(ang), zeros], axis=1)
    return cos, sin


def _pad_cols(a, width):
    return jnp.pad(a, [(0, 0)] * (a.ndim - 1) + [(0, width - a.shape[-1])])


def _forward(x, positions, attn_norm, w_in, hg_lower_bounds, hg_out_norm,
             mla_q_a_norm, mla_w_qb, mla_kv_a_norm, mla_w_kvb, mla_q_norm, mla_k_norm,
             rw_mu, rw_w0, rw_w2, rw_a0, rw_a2, rw_g2, rw_v0, rw_v1, rw_v2,
             rw_k_k, rw_k_a, rw_r_k, rw_ln_w, rw_ln_b,
             w_o, ffn_norm, w_gate_up, w_down, *, cfg):
    batch, seq, d = x.shape
    depth = w_in.shape[0]
    t = batch * seq
    x = x.reshape(t, d)

    lb_sm = jax.nn.softmax(hg_lower_bounds.astype(F32), axis=0)
    lower_bounds = jnp.cumsum(lb_sm, axis=0) - lb_sm[0]
    cos, sin = _rope_tables(positions)

    hg_end = 4 * HG_WIDTH
    mla_end = hg_end + MLA_IN
    w_in_hg = w_in[:, :, :hg_end].astype(BF16)
    w_in_mla = _pad_cols(w_in[:, :, hg_end:mla_end], MLA_IN_PAD).astype(BF16)
    w_in_rw = w_in[:, :, mla_end:].astype(BF16)
    w_qb = _pad_cols(mla_w_qb.reshape(depth, MLA_Q_RANK, MLA_HEADS, MLA_QK_DIM), MLA_QK_PAD)
    w_qb = w_qb.reshape(depth, MLA_Q_RANK, MLA_HEADS * MLA_QK_PAD).astype(BF16)
    w_kvb = mla_w_kvb.astype(BF16)
    split_gain = lambda g: jnp.stack([g[:, :MLA_NOPE], _pad_cols(g[:, MLA_NOPE:], LANES)], axis=1)
    g_q = split_gain(mla_q_norm)
    g_k = split_gain(mla_k_norm)
    w_o_hg = w_o[:, :HG_WIDTH].astype(BF16)
    w_o_mla = w_o[:, HG_WIDTH:HG_WIDTH + MLA_WIDTH].astype(BF16)
    w_o_rw = w_o[:, HG_WIDTH + MLA_WIDTH:].astype(BF16)
    w_gu = w_gate_up.astype(BF16)
    w_dn = w_down.astype(BF16)
    rw_w2b, rw_a2b, rw_g2b = rw_w2.astype(BF16), rw_a2.astype(BF16), rw_g2.astype(BF16)
    rw_v1b, rw_v2b = rw_v1.astype(BF16), rw_v2.astype(BF16)
    r_k = rw_r_k.reshape(depth, RW_WIDTH)

    v_first = None
    for l in range(depth):
        p_rw, q, k, v, o_hg = mixer_in(x, attn_norm[l], w_in_hg[l], w_in_mla[l], w_in_rw[l], cos, sin,
                                       mla_q_a_norm[l], w_qb[l], mla_kv_a_norm[l], w_kvb[l], g_q[l], g_k[l],
                                       lower_bounds[l], hg_out_norm[l], batch=batch, bm=cfg["proj_bm"])
        o_mla = flash_attn(q, k, v, batch=batch, blk=cfg["attn_blk"])

        vres = None if l == 0 else (rw_v0[l - 1], rw_v1b[l - 1], rw_v2b[l - 1], v_first)
        outs = rw_chunk(p_rw, rw_mu[l], rw_w0[l], rw_w2b[l], rw_a0[l], rw_a2b[l], rw_g2b[l],
                        rw_k_k[l], rw_k_a[l], r_k[l], vres, batch=batch, bt=cfg["rw_bt"])
        if l == 0:
            v_first = outs[7]
        x = out_mix(x, o_hg, o_mla, w_o_hg[l], w_o_mla[l], w_o_rw[l], *outs[:7], rw_ln_w[l], rw_ln_b[l],
                    batch=batch, bs=cfg["out_bm"])
        x = ffn(x, ffn_norm[l], w_gu[l], w_dn[l], bm=cfg["ffn_bm"], bf=cfg["ffn_bf"])
    return x.reshape(batch, seq, d)


_CFG = dict(proj_bm=256, attn_blk=1024, rw_bt=256,
            out_bm=512, ffn_bm=1024, ffn_bf=512)


def kernel(x, positions, attn_norm, w_in, hg_lower_bounds, hg_out_norm, mla_q_a_norm, mla_w_qb, mla_kv_a_norm,
           mla_w_kvb, mla_q_norm, mla_k_norm, rw_mu, rw_w0, rw_w2, rw_a0, rw_a2, rw_g2, rw_v0, rw_v1, rw_v2,
           rw_k_k, rw_k_a, rw_r_k, rw_ln_w, rw_ln_b, w_o, ffn_norm, w_gate_up, w_down):
    return _forward(x, positions, attn_norm, w_in, hg_lower_bounds, hg_out_norm, mla_q_a_norm, mla_w_qb,
                    mla_kv_a_norm, mla_w_kvb, mla_q_norm, mla_k_norm, rw_mu, rw_w0, rw_w2, rw_a0, rw_a2, rw_g2,
                    rw_v0, rw_v1, rw_v2, rw_k_k, rw_k_a, rw_r_k, rw_ln_w, rw_ln_b, w_o, ffn_norm, w_gate_up,
                    w_down, cfg=_CFG)
```
